```python
import jax, jax.numpy as jnp
from jax import lax
import numpy as np

D_MODEL = 1024
BATCH = 4
SEQ = 8192
DEPTH = 4

RET_HEADS = 4
RET_QK_DIM = 64
RET_V_DIM = 128
RET_CHUNK = 128
MLA_HEADS = 4
MLA_Q_RANK = 384
MLA_KV_RANK = 256
MLA_NOPE_DIM = 128
MLA_ROPE_DIM = 64
MLA_V_DIM = 128
ATTN_BLOCK = 128
ROPE_DIM = 64
ROPE_BASE = 10000.0
CONV_WIDTH = D_MODEL
CONV_KERNEL = 31
EPS = 1e-6

RET_WIDTH = RET_HEADS * RET_V_DIM
MLA_WIDTH = MLA_HEADS * MLA_V_DIM
MIX_WIDTH = RET_WIDTH + MLA_WIDTH
EV_SPLITS = (RET_HEADS * RET_QK_DIM, RET_HEADS * RET_QK_DIM, RET_WIDTH, RET_WIDTH,
             MLA_Q_RANK, MLA_KV_RANK, MLA_ROPE_DIM, MLA_WIDTH)
EV_IN_WIDTH = 2 * RET_HEADS * RET_QK_DIM + 2 * RET_WIDTH + MLA_Q_RANK + MLA_KV_RANK + MLA_ROPE_DIM + MLA_WIDTH
OD_IN_WIDTH = 3 * CONV_WIDTH
N_EVEN = (DEPTH + 1) // 2
N_ODD = DEPTH // 2

kernel_name = 'hybrid_retention_mla_conformer_encoder'


def _split(z, sizes):
    outs = []
    start = 0
    for size in sizes:
        outs.append(z[..., start:start + size])
        start += size
    return outs


def _rmsnorm(x, g):
    xf = x.astype(jnp.float32)
    y = xf * lax.rsqrt(jnp.mean(xf * xf, axis=-1, keepdims=True) + EPS)
    return (y * g.astype(jnp.float32)).astype(x.dtype)


def _layernorm(x, g, b):
    xf = x.astype(jnp.float32)
    mu = jnp.mean(xf, axis=-1, keepdims=True)
    var = jnp.mean(jnp.square(xf - mu), axis=-1, keepdims=True)
    y = (xf - mu) * lax.rsqrt(var + EPS)
    return (y * g.astype(jnp.float32) + b.astype(jnp.float32)).astype(x.dtype)


def _rope_tables(positions):
    inv_freq = ROPE_BASE ** (-jnp.arange(0, ROPE_DIM, 2, dtype=jnp.float32) / ROPE_DIM)
    ang = positions.astype(jnp.float32)[..., None] * inv_freq
    return jnp.cos(ang)[:, :, None, :], jnp.sin(ang)[:, :, None, :]


def _rope(x, cos, sin):
    xf = x.astype(jnp.float32)
    x1, x2 = xf[..., :ROPE_DIM // 2], xf[..., ROPE_DIM // 2:]
    return jnp.concatenate([x1 * cos - x2 * sin, x2 * cos + x1 * sin], axis=-1).astype(x.dtype)


def _retention_dir(q, k, v, log_gamma, include_diag):
    B, S, H, dk = q.shape
    dv = v.shape[-1]
    C = RET_CHUNK
    N = S // C
    qc = q.reshape(B, N, C, H, dk)
    kc = k.reshape(B, N, C, H, dk)
    vc = v.reshape(B, N, C, H, dv)
    idx = jnp.arange(C, dtype=jnp.float32)
    diff = idx[:, None] - idx[None, :]
    mask = (diff >= 0) if include_diag else (diff > 0)
    decay = jnp.where(mask[None], jnp.exp(log_gamma[:, None, None] * jnp.maximum(diff, 0.0)[None]), 0.0)
    scores = jnp.einsum('bnihd,bnjhd->bnhij', qc, kc) * decay
    inner = jnp.einsum('bnhij,bnjhe->bnihe', scores, vc)
    k_w = jnp.exp(log_gamma[None, :] * (C - 1 - idx)[:, None])
    kv = jnp.einsum('bnjhd,jh,bnjhe->nbhde', kc, k_w, vc)
    chunk_decay = jnp.exp(log_gamma * C)[None, :, None, None]

    def step(state, kv_n):
        return state * chunk_decay + kv_n, state

    _, states = lax.scan(step, jnp.zeros((B, H, dk, dv), jnp.float32), kv)
    q_w = jnp.exp(log_gamma[None, :] * (idx + 1.0)[:, None])
    cross = jnp.einsum('bnihd,ih,nbhde->bnihe', qc, q_w, states)
    return (inner + cross).reshape(B, S, H, dv)


def _dense_attention_blocks(q, k, v):
    B, S, H, dq = q.shape
    dv = v.shape[-1]
    nb = S // ATTN_BLOCK
    qb = q.reshape(B, nb, ATTN_BLOCK, H, dq).transpose(1, 0, 2, 3, 4)
    scale = dq ** -0.5

    def one(qblk):
        s = jnp.einsum('bqhd,bkhd->bhqk', qblk, k, preferred_element_type=jnp.float32) * scale
        p = jax.nn.softmax(s, axis=-1).astype(v.dtype)
        return jnp.einsum('bhqk,bkhd->bqhd', p, v)

    o = lax.map(one, qb)
    return o.transpose(1, 0, 2, 3, 4).reshape(B, S, H, dv)


def _even_mixer(h, cos, sin, w_in, dec_f, dec_b, q_norm_g, w_uq, kv_norm_g, w_ukv, w_out):
    B, S, _ = h.shape
    z = h @ w_in
    rq, rk, rv, rg, cq, ckv, kr, mg = _split(z, EV_SPLITS)
    rq = _rope(rq.reshape(B, S, RET_HEADS, RET_QK_DIM), cos, sin).astype(jnp.float32)
    rk = _rope(rk.reshape(B, S, RET_HEADS, RET_QK_DIM), cos, sin).astype(jnp.float32) * (RET_QK_DIM ** -0.5)
    rv = rv.reshape(B, S, RET_HEADS, RET_V_DIM).astype(jnp.float32)
    lg_f = jax.nn.log_sigmoid(dec_f.astype(jnp.float32))
    lg_b = jax.nn.log_sigmoid(dec_b.astype(jnp.float32))
    o_f = _retention_dir(rq, rk, rv, lg_f, True)
    o_b = jnp.flip(_retention_dir(jnp.flip(rq, 1), jnp.flip(rk, 1), jnp.flip(rv, 1), lg_b, False), 1)
    o = o_f + o_b
    mu = jnp.mean(o, axis=-1, keepdims=True)
    var = jnp.mean(jnp.square(o - mu), axis=-1, keepdims=True)
    o = (o - mu) * lax.rsqrt(var + EPS)
    ret_out = o.reshape(B, S, RET_WIDTH).astype(h.dtype) * jax.nn.silu(rg)
    q = (_rmsnorm(cq, q_norm_g) @ w_uq).reshape(B, S, MLA_HEADS, MLA_NOPE_DIM + MLA_ROPE_DIM)
    q = jnp.concatenate([q[..., :MLA_NOPE_DIM], _rope(q[..., MLA_NOPE_DIM:], cos, sin)], axis=-1)
    kv = (_rmsnorm(ckv, kv_norm_g) @ w_ukv).reshape(B, S, MLA_HEADS, MLA_NOPE_DIM + MLA_V_DIM)
    k_nope, v = kv[..., :MLA_NOPE_DIM], kv[..., MLA_NOPE_DIM:]
    k_rope = _rope(kr[:, :, None, :], cos, sin)
    k = jnp.concatenate([k_nope, jnp.broadcast_to(k_rope, (B, S, MLA_HEADS, MLA_ROPE_DIM))], axis=-1)
    a = _dense_attention_blocks(q, k, v)
    mla_out = a.reshape(B, S, MLA_WIDTH) * jax.nn.silu(mg)
    return jnp.concatenate([ret_out, mla_out], axis=-1) @ w_out


def _odd_mixer(h, w_in, b_in, dw_w, dw_b, ln_g, ln_b, w_out):
    z = h @ w_in + b_in
    a, b, g = _split(z, (CONV_WIDTH, CONV_WIDTH, CONV_WIDTH))
    u = a * jax.nn.sigmoid(b)
    u = lax.conv_general_dilated(
        u, dw_w[:, None, :].astype(u.dtype), window_strides=(1,),
        padding=((CONV_KERNEL // 2, CONV_KERNEL // 2),),
        dimension_numbers=('NWC', 'WIO', 'NWC'),
        feature_group_count=CONV_WIDTH) + dw_b
    u = jax.nn.silu(_layernorm(u, ln_g, ln_b))
    return (u * jax.nn.silu(g)) @ w_out


def setup_inputs(seed: int = 0) -> dict:
    key = jax.random.key(seed)
    ks = jax.random.split(key, 24)
    f32 = jnp.float32

    def nrm(k, shape, fan_in):
        return jax.random.normal(k, shape, f32) * (fan_in ** -0.5)

    def small(k, shape):
        return 0.02 * jax.random.normal(k, shape, f32)

    def gain(k, shape):
        return 1.0 + 0.02 * jax.random.normal(k, shape, f32)

    hidx = jnp.arange(RET_HEADS, dtype=f32)
    g0 = 1.0 - 2.0 ** (-5.0 - hidx)
    base_logit = jnp.log(g0) - jnp.log1p(-g0)
    return {
        'x': jax.random.normal(ks[0], (BATCH, SEQ, D_MODEL), f32),
        'c': jax.random.normal(ks[1], (BATCH, D_MODEL), f32),
        'positions': jnp.arange(SEQ, dtype=jnp.int32)[None, :]
                     + jax.random.randint(ks[2], (BATCH, 1), 0, SEQ, dtype=jnp.int32),
        'ada_w': 0.5 * nrm(ks[3], (DEPTH, D_MODEL, 3 * D_MODEL), D_MODEL),
        'ada_b': small(ks[4], (DEPTH, 3 * D_MODEL)),
        'pre_g': gain(ks[5], (DEPTH, D_MODEL)),
        'post_g': gain(ks[6], (DEPTH, D_MODEL)),
        'ev_w_in': nrm(ks[7], (N_EVEN, D_MODEL, EV_IN_WIDTH), D_MODEL),
        'ev_dec_f': base_logit[None, :] + 0.1 * jax.random.normal(ks[8], (N_EVEN, RET_HEADS), f32),
        'ev_dec_b': base_logit[None, :] + 0.1 * jax.random.normal(ks[9], (N_EVEN, RET_HEADS), f32),
        'ev_q_norm_g': gain(ks[10], (N_EVEN, MLA_Q_RANK)),
        'ev_w_uq': nrm(ks[11], (N_EVEN, MLA_Q_RANK, MLA_HEADS * (MLA_NOPE_DIM + MLA_ROPE_DIM)), MLA_Q_RANK),
        'ev_kv_norm_g': gain(ks[12], (N_EVEN, MLA_KV_RANK)),
        'ev_w_ukv': nrm(ks[13], (N_EVEN, MLA_KV_RANK, MLA_HEADS * (MLA_NOPE_DIM + MLA_V_DIM)), MLA_KV_RANK),
        'ev_w_out': nrm(ks[14], (N_EVEN, MIX_WIDTH, D_MODEL), MIX_WIDTH),
        'od_w_in': nrm(ks[15], (N_ODD, D_MODEL, OD_IN_WIDTH), D_MODEL),
        'od_b_in': small(ks[16], (N_ODD, OD_IN_WIDTH)),
        'od_dw_w': nrm(ks[17], (N_ODD, CONV_KERNEL, CONV_WIDTH), CONV_KERNEL),
        'od_dw_b': small(ks[18], (N_ODD, CONV_WIDTH)),
        'od_ln_g': gain(ks[19], (N_ODD, CONV_WIDTH)),
        'od_ln_b': small(ks[20], (N_ODD, CONV_WIDTH)),
        'od_w_out': nrm(ks[21], (N_ODD, CONV_WIDTH, D_MODEL), CONV_WIDTH),
    }


def reference(x, c, positions, ada_w, ada_b, pre_g, post_g,
              ev_w_in, ev_dec_f, ev_dec_b, ev_q_norm_g, ev_w_uq, ev_kv_norm_g, ev_w_ukv, ev_w_out,
              od_w_in, od_b_in, od_dw_w, od_dw_b, od_ln_g, od_ln_b, od_w_out):
    cos, sin = _rope_tables(positions)
    c_act = jax.nn.silu(c)
    for layer in range(DEPTH):
        mod = c_act @ ada_w[layer] + ada_b[layer]
        shift, scale, gate = _split(mod, (D_MODEL, D_MODEL, D_MODEL))
        h = _rmsnorm(x, pre_g[layer]) * (1.0 + scale[:, None, :]) + shift[:, None, :]
        if layer % 2 == 0:
            i = layer // 2
            y = _even_mixer(h, cos, sin, ev_w_in[i], ev_dec_f[i], ev_dec_b[i], ev_q_norm_g[i],
                            ev_w_uq[i], ev_kv_norm_g[i], ev_w_ukv[i], ev_w_out[i])
        else:
            i = layer // 2
            y = _odd_mixer(h, od_w_in[i], od_b_in[i], od_dw_w[i], od_dw_b[i],
                           od_ln_g[i], od_ln_b[i], od_w_out[i])
        x = x + gate[:, None, :] * _rmsnorm(y, post_g[layer])
    return x
```

```python
import functools

import jax
import jax.numpy as jnp
from jax import lax
from jax.experimental import pallas as pl
from jax.experimental.pallas import tpu as pltpu

F32 = jnp.float32
BF16 = jnp.bfloat16

RET_HEADS = 4
RET_QK_DIM = 64
RET_V_DIM = 128
RET_CHUNK = 128
MLA_HEADS = 4
MLA_Q_RANK = 384
MLA_KV_RANK = 256
MLA_NOPE_DIM = 128
MLA_ROPE_DIM = 64
MLA_V_DIM = 128
ROPE_DIM = 64
ROPE_BASE = 10000.0
CONV_KERNEL = 31
EPS = 1e-6

RET_WIDTH = RET_HEADS * RET_V_DIM
MLA_WIDTH = MLA_HEADS * MLA_V_DIM
RET_SCALE = RET_QK_DIM ** -0.5
ATT_SCALE = (MLA_NOPE_DIM + MLA_ROPE_DIM) ** -0.5

LANES = 128
MXU_DIM = 256
ATT_QK_PAD = MXU_DIM
CONV_HALO = 16

TOKEN_TILE = 512
ATT_Q_TILE = 256
ATT_KV_TILE = TOKEN_TILE
CONV_ROW_CHUNK = 32

V7X_VMEM_BYTES = 64 * 1024 * 1024
VMEM_LIMIT_BYTES = V7X_VMEM_BYTES - 8 * 1024 * 1024

_C_RQ = 0
_C_RK = _C_RQ + RET_HEADS * RET_QK_DIM
_C_RV = _C_RK + RET_HEADS * RET_QK_DIM
_C_RG = _C_RV + RET_WIDTH
_C_CQ = _C_RG + RET_WIDTH
_C_CKV = _C_CQ + MLA_Q_RANK
_C_KR = _C_CKV + MLA_KV_RANK
_C_MG = _C_KR + LANES
_C_END = _C_MG + MLA_WIDTH


def _cparams(semantics):
    return pltpu.CompilerParams(dimension_semantics=semantics, vmem_limit_bytes=VMEM_LIMIT_BYTES)


def _const_spec(shape):
    nd = len(shape)
    return pl.BlockSpec(shape, lambda *_: (0,) * nd, pipeline_mode=pl.Buffered(1))


def _silu(v):
    return v * jax.nn.sigmoid(v)


def _rms(v, g):
    return v * lax.rsqrt(jnp.mean(v * v, axis=-1, keepdims=True) + EPS) * g


def _mod_kernel(c_ref, w_ref, b_ref, o_ref):
    c = c_ref[...]
    o_ref[...] = jnp.dot(_silu(c).astype(BF16), w_ref[...].astype(BF16),
                         preferred_element_type=F32) + b_ref[...]


def _modulation(c, ada_w, ada_b):
    depth, d, d3 = ada_w.shape
    b = c.shape[0]
    rows = -(-b // 8) * 8
    c_pad = jnp.pad(c, ((0, rows - b), (0, 0)))
    out = pl.pallas_call(
        _mod_kernel,
        grid=(depth, d3 // d),
        in_specs=[pl.BlockSpec((rows, d), lambda l, j: (0, 0)),
                  pl.BlockSpec((None, d, d), lambda l, j: (l, 0, j)),
                  pl.BlockSpec((None, 1, d), lambda l, j: (l, 0, j))],
        out_specs=pl.BlockSpec((None, rows, d), lambda l, j: (l, 0, j)),
        out_shape=jax.ShapeDtypeStruct((depth, rows, d3), F32),
        compiler_params=_cparams(("arbitrary", "arbitrary")),
        name="adaln_mod",
    )(c_pad, ada_w, ada_b.reshape(depth, 1, d3))
    return out[:, :b].reshape(depth, b, d3 // d, d)


def _rope_table_kernel(pos_ref, invf_ref, sgn_ref, cos_ref, sin_ref):
    ang = pos_ref[...].astype(F32) * invf_ref[...]
    cos_ref[...] = jnp.cos(ang)
    sin_ref[...] = jnp.sin(ang) * sgn_ref[...]


def _rope_tables(positions):
    b, s = positions.shape
    half = ROPE_DIM // 2
    inv_freq = ROPE_BASE ** (-jnp.arange(0, ROPE_DIM, 2, dtype=F32) / ROPE_DIM)
    invf = jnp.tile(inv_freq, LANES // half).reshape(1, LANES)
    sgn = jnp.tile(jnp.concatenate([-jnp.ones((half,), F32), jnp.ones((half,), F32)]),
                   LANES // ROPE_DIM).reshape(1, LANES)
    ts = min(TOKEN_TILE, s)
    return pl.pallas_call(
        _rope_table_kernel,
        grid=(b, s // ts),
        in_specs=[pl.BlockSpec((None, ts, 1), lambda i, t: (i, t, 0)),
                  pl.BlockSpec((1, LANES), lambda i, t: (0, 0)),
                  pl.BlockSpec((1, LANES), lambda i, t: (0, 0))],
        out_specs=[pl.BlockSpec((None, ts, LANES), lambda i, t: (i, t, 0))] * 2,
        out_shape=[jax.ShapeDtypeStruct((b, s, LANES), F32)] * 2,
        compiler_params=_cparams(("arbitrary", "arbitrary")),
        name="rope_tables",
    )(positions.reshape(b, s, 1), invf, sgn)


def _rope_slab(v, cos, sin_signed, first_half):
    half = ROPE_DIM // 2
    partner = jnp.where(first_half, pltpu.roll(v, LANES - half, 1), pltpu.roll(v, half, 1))
    return v * cos + partner * sin_signed


def _even_in_kernel(x_ref, mod_ref, pg_ref, win_ref, qg_ref, wuq_ref, kvg_ref, wukv_ref,
                    cos_ref, sin_ref,
                    rq_ref, rk_ref, rv_ref, gate_ref, qt_ref, kc_ref, vt_ref):
    x = x_ref[...]
    h = _rms(x, pg_ref[...]) * (1.0 + mod_ref[1:2, :]) + mod_ref[0:1, :]
    z = jnp.dot(h.astype(BF16), win_ref[...], preferred_element_type=F32)

    cos = cos_ref[...]
    sin = sin_ref[...]
    lane = lax.broadcasted_iota(jnp.int32, cos.shape, 1)
    first_half = (lane % ROPE_DIM) < (ROPE_DIM // 2)
    rope = functools.partial(_rope_slab, cos=cos, sin_signed=sin, first_half=first_half)

    for j in range(RET_HEADS * RET_QK_DIM // LANES):
        sl = slice(j * LANES, (j + 1) * LANES)
        rq_ref[:, sl] = rope(z[:, _C_RQ + j * LANES:_C_RQ + (j + 1) * LANES]).astype(BF16)
        rk_ref[:, sl] = (rope(z[:, _C_RK + j * LANES:_C_RK + (j + 1) * LANES]) * RET_SCALE).astype(BF16)
    rv_ref[...] = z[:, _C_RV:_C_RG].astype(BF16)
    gate_ref[:, :RET_WIDTH] = _silu(z[:, _C_RG:_C_CQ]).astype(BF16)
    gate_ref[:, RET_WIDTH:] = _silu(z[:, _C_MG:_C_END]).astype(BF16)

    cqn = _rms(z[:, _C_CQ:_C_CKV], qg_ref[...])
    q = jnp.dot(cqn.astype(BF16), wuq_ref[...], preferred_element_type=F32)
    kvn = _rms(z[:, _C_CKV:_C_KR], kvg_ref[...])
    kv = jnp.dot(kvn.astype(BF16), wukv_ref[...], preferred_element_type=F32)

    k_rope = rope(z[:, _C_KR:_C_MG]).astype(BF16)
    nope_w = MLA_HEADS * MLA_NOPE_DIM
    q_rope = [rope(q[:, nope_w + j * LANES:nope_w + (j + 1) * LANES]) for j in range(2)]
    low = lane < ROPE_DIM
    for hd in range(MLA_HEADS):
        qn = q[:, hd * MLA_NOPE_DIM:(hd + 1) * MLA_NOPE_DIM] * ATT_SCALE
        slab = q_rope[hd // 2]
        if hd % 2:
            slab = pltpu.roll(slab, ROPE_DIM, 1)
        qr = jnp.where(low, slab, 0.0) * ATT_SCALE
        qt_ref[hd, :MLA_NOPE_DIM, :] = qn.T.astype(BF16)
        qt_ref[hd, MLA_NOPE_DIM:, :] = qr.T.astype(BF16)
        base = hd * (MLA_NOPE_DIM + MLA_V_DIM)
        kc_ref[hd, :, :MLA_NOPE_DIM] = kv[:, base:base + MLA_NOPE_DIM].astype(BF16)
        kc_ref[hd, :, MLA_NOPE_DIM:] = k_rope
        vt_ref[hd, 0] = kv[:, base + MLA_NOPE_DIM:base + MLA_NOPE_DIM + MLA_V_DIM].T.astype(BF16)


def _even_in(x, mod, pre_g, w_in_p, q_norm_g, w_uq_p, kv_norm_g, w_ukv, cos_t, sin_t):
    b, s, d = x.shape
    ts = TOKEN_TILE
    nt = s // ts
    hq = RET_HEADS * RET_QK_DIM
    tile3 = lambda w: pl.BlockSpec((None, ts, w), lambda i, t: (i, t, 0))
    out_shapes = [
        jax.ShapeDtypeStruct((b, s, hq), BF16),
        jax.ShapeDtypeStruct((b, s, hq), BF16),
        jax.ShapeDtypeStruct((b, s, RET_WIDTH), BF16),
        jax.ShapeDtypeStruct((b, s, RET_WIDTH + MLA_WIDTH), BF16),
        jax.ShapeDtypeStruct((b, MLA_HEADS, ATT_QK_PAD, s), BF16),
        jax.ShapeDtypeStruct((b, MLA_HEADS, s, ATT_QK_PAD), BF16),
        jax.ShapeDtypeStruct((b, MLA_HEADS, nt, MLA_V_DIM, ts), BF16),
    ]
    out_specs = [
        tile3(hq), tile3(hq), tile3(RET_WIDTH), tile3(RET_WIDTH + MLA_WIDTH),
        pl.BlockSpec((None, MLA_HEADS, ATT_QK_PAD, ts), lambda i, t: (i, 0, 0, t)),
        pl.BlockSpec((None, MLA_HEADS, ts, ATT_QK_PAD), lambda i, t: (i, 0, t, 0)),
        pl.BlockSpec((None, MLA_HEADS, 1, MLA_V_DIM, ts), lambda i, t: (i, 0, t, 0, 0)),
    ]
    return pl.pallas_call(
        _even_in_kernel,
        grid=(b, nt),
        in_specs=[tile3(d),
                  pl.BlockSpec((None, 3, d), lambda i, t: (i, 0, 0)),
                  _const_spec((1, d)),
                  _const_spec(w_in_p.shape),
                  _const_spec((1, MLA_Q_RANK)),
                  _const_spec(w_uq_p.shape),
                  _const_spec((1, MLA_KV_RANK)),
                  _const_spec(w_ukv.shape),
                  tile3(LANES), tile3(LANES)],
        out_specs=out_specs,
        out_shape=out_shapes,
        compiler_params=_cparams(("arbitrary", "arbitrary")),
        name="even_in",
    )(x, mod, pre_g.reshape(1, d), w_in_p, q_norm_g.reshape(1, -1), w_uq_p,
      kv_norm_g.reshape(1, -1), w_ukv, cos_t, sin_t)


def _retention_kernel(dec_ref, q_ref, k_ref, v_ref, o_ref, sf_ref, kvb_ref, *, n_chunks):
    c = RET_CHUNK
    dk = RET_QK_DIM
    lg = jax.nn.log_sigmoid(dec_ref[...])
    ri = lax.broadcasted_iota(jnp.int32, (c, LANES), 0)
    ci = lax.broadcasted_iota(jnp.int32, (c, LANES), 1)
    rif = ri.astype(F32)
    diff = rif - ci.astype(F32)
    row_h0 = ri < dk
    lane_h0 = ci < dk

    def dmat(lgf, lgb):
        return jnp.where(diff >= 0, jnp.exp(lgf * jnp.maximum(diff, 0.0)), jnp.exp(lgb * jnp.maximum(-diff, 0.0)))

    dec = jnp.concatenate([dmat(lg[0:1], lg[2:3]), dmat(lg[1:2], lg[3:4])], axis=0)
    qwf = jnp.concatenate([jnp.exp(lg[0:1] * (rif + 1.0)), jnp.exp(lg[1:2] * (rif + 1.0))], axis=0)
    qwb = jnp.concatenate([jnp.exp(lg[2:3] * (c - rif)), jnp.exp(lg[3:4] * (c - rif))], axis=0)
    lgf_lane = jnp.where(lane_h0, lg[0:1], lg[1:2])
    lgb_lane = jnp.where(lane_h0, lg[2:3], lg[3:4])
    kwf = jnp.exp(lgf_lane * (c - 1.0 - rif))
    kwb = jnp.exp(lgb_lane * rif)
    cdf = jnp.where(row_h0, jnp.exp(lg[0:1] * c), jnp.exp(lg[1:2] * c))
    cdb = jnp.where(row_h0, jnp.exp(lg[2:3] * c), jnp.exp(lg[3:4] * c))

    def rows_of(n):
        return pl.ds(pl.multiple_of(n * c, c), c)

    def pick_heads(m):
        return jnp.where(row_h0, m[:, :RET_V_DIM], m[:, RET_V_DIM:])

    def fwd_body(n, state_f):
        rows = rows_of(n)
        k = k_ref[rows, :].astype(F32)
        kw_t = jnp.concatenate([k * kwf, k * kwb], axis=1).T.astype(BF16)
        kv = jnp.dot(kw_t, v_ref[rows, :], preferred_element_type=F32)
        sf_ref[n] = state_f.astype(BF16)
        kvb_ref[n] = pick_heads(kv[c:, :])
        return state_f * cdf + pick_heads(kv[:c, :])

    lax.fori_loop(0, n_chunks, fwd_body, jnp.zeros((c, LANES), F32))

    def norm(o):
        mu = jnp.mean(o, axis=-1, keepdims=True)
        var = jnp.mean(jnp.square(o - mu), axis=-1, keepdims=True)
        return (o - mu) * lax.rsqrt(var + EPS)

    def bwd_body(t, state_b):
        n = n_chunks - 1 - t
        rows = rows_of(n)
        q = q_ref[rows, :].astype(F32)
        qm = jnp.concatenate([jnp.where(lane_h0, q, 0.0), jnp.where(lane_h0, 0.0, q)], axis=0)
        s = lax.dot_general(qm.astype(BF16), k_ref[rows, :], (((1,), (1,)), ((), ())),
                            preferred_element_type=F32)
        p = (s * dec).astype(BF16)
        v = v_ref[rows, :]
        inner0 = jnp.dot(p[:c], v[:, :RET_V_DIM], preferred_element_type=F32)
        inner1 = jnp.dot(p[c:], v[:, RET_V_DIM:], preferred_element_type=F32)
        lhs = jnp.concatenate([qm * qwf, qm * qwb], axis=1).astype(BF16)
        rhs = jnp.concatenate([sf_ref[n], state_b.astype(BF16)], axis=0)
        cross = jnp.dot(lhs, rhs, preferred_element_type=F32)
        o_ref[rows, :RET_V_DIM] = norm(inner0 + cross[:c]).astype(BF16)
        o_ref[rows, RET_V_DIM:] = norm(inner1 + cross[c:]).astype(BF16)
        return state_b * cdb + kvb_ref[n]

    lax.fori_loop(0, n_chunks, bwd_body, jnp.zeros((c, LANES), F32))


def _retention(rq, rk, rv, dec_f, dec_b):
    b, s, _ = rq.shape
    n_chunks = s // RET_CHUNK
    pairs = RET_HEADS // 2
    dec = jnp.concatenate([dec_f.reshape(pairs, 2), dec_b.reshape(pairs, 2)], axis=1)
    dec = jnp.broadcast_to(dec[:, :, None], (pairs, 4, LANES)).astype(F32)
    seq = lambda w: pl.BlockSpec((None, s, w), lambda i, p: (i, 0, p))
    return pl.pallas_call(
        functools.partial(_retention_kernel, n_chunks=n_chunks),
        grid=(b, pairs),
        in_specs=[pl.BlockSpec((None, 4, LANES), lambda i, p: (p, 0, 0)),
                  seq(2 * RET_QK_DIM), seq(2 * RET_QK_DIM), seq(2 * RET_V_DIM)],
        out_specs=seq(2 * RET_V_DIM),
        out_shape=jax.ShapeDtypeStruct((b, s, RET_WIDTH), BF16),
        scratch_shapes=[pltpu.VMEM((n_chunks, RET_CHUNK, LANES), BF16),
                        pltpu.VMEM((n_chunks, RET_CHUNK, LANES), F32)],
        compiler_params=_cparams(("arbitrary", "arbitrary")),
        name="retention",
    )(dec, rq, rk, rv)


def _attention_kernel(qt_ref, kc_ref, vt_ref, o_ref, *, n_kv):
    qt = qt_ref[...]
    tq = qt.shape[1]
    tk = ATT_KV_TILE

    def body(j, carry):
        m, l, acc = carry
        k = kc_ref[pl.ds(pl.multiple_of(j * tk, tk), tk), :]
        s = jnp.dot(k, qt, preferred_element_type=F32)
        m_new = jnp.maximum(m, jnp.max(s, axis=0, keepdims=True))
        alpha = jnp.exp(m - m_new)
        p = jnp.exp(s - m_new)
        l = alpha * l + jnp.sum(p, axis=0, keepdims=True)
        acc = alpha * acc + jnp.dot(vt_ref[j], p.astype(BF16), preferred_element_type=F32)
        return m_new, l, acc

    init = (jnp.full((1, tq), -1e30, F32), jnp.zeros((1, tq), F32), jnp.zeros((MLA_V_DIM, tq), F32))
    _, l, acc = lax.fori_loop(0, n_kv, body, init)
    o_ref[...] = (acc / l).T.astype(BF16)


def _attention(qt, kc, vt):
    b, h, _, s = qt.shape
    n_kv = vt.shape[2]
    tq = ATT_Q_TILE
    return pl.pallas_call(
        functools.partial(_attention_kernel, n_kv=n_kv),
        grid=(b, h, s // tq),
        in_specs=[pl.BlockSpec((None, None, ATT_QK_PAD, tq), lambda i, j, t: (i, j, 0, t)),
                  pl.BlockSpec((None, None, s, ATT_QK_PAD), lambda i, j, t: (i, j, 0, 0)),
                  pl.BlockSpec((None, None, n_kv, MLA_V_DIM, ATT_KV_TILE), lambda i, j, t: (i, j, 0, 0, 0))],
        out_specs=pl.BlockSpec((None, tq, MLA_V_DIM), lambda i, j, t: (i, t, j)),
        out_shape=jax.ShapeDtypeStruct((b, s, MLA_WIDTH), BF16),
        compiler_params=_cparams(("arbitrary", "arbitrary", "arbitrary")),
        name="mla_attention",
    )(qt, kc, vt)


def _even_out_kernel(x_ref, ret_ref, att_ref, gate_ref, w_ref, pg_ref, mod_ref, o_ref):
    m_ret = ret_ref[...] * gate_ref[:, :RET_WIDTH]
    m_att = att_ref[...] * gate_ref[:, RET_WIDTH:]
    y = (jnp.dot(m_ret, w_ref[:RET_WIDTH, :], preferred_element_type=F32)
         + jnp.dot(m_att, w_ref[RET_WIDTH:, :], preferred_element_type=F32))
    o_ref[...] = x_ref[...] + mod_ref[2:3, :] * _rms(y, pg_ref[...])


def _even_out(x, ret, att, gates, w_out, post_g, mod):
    b, s, d = x.shape
    ts = TOKEN_TILE
    tile3 = lambda w: pl.BlockSpec((None, ts, w), lambda i, t: (i, t, 0))
    return pl.pallas_call(
        _even_out_kernel,
        grid=(b, s // ts),
        in_specs=[tile3(d), tile3(RET_WIDTH), tile3(MLA_WIDTH), tile3(RET_WIDTH + MLA_WIDTH),
                  _const_spec(w_out.shape), _const_spec((1, d)),
                  pl.BlockSpec((None, 3, d), lambda i, t: (i, 0, 0))],
        out_specs=tile3(d),
        out_shape=jax.ShapeDtypeStruct((b, s, d), F32),
        compiler_params=_cparams(("arbitrary", "arbitrary")),
        name="even_out",
    )(x, ret, att, gates, w_out, post_g.reshape(1, d), mod)


def _odd_kernel(xp_ref, x_ref, xn_ref, mod_ref, pg_ref, win_ref, bin_ref, dww_ref, dwb_ref,
                lng_ref, lnb_ref, wout_ref, postg_ref, o_ref, u_ref, c_ref, *, n_tiles):
    t = pl.program_id(1)
    ts, d = x_ref.shape
    halo = CONV_HALO
    x = x_ref[...]
    xa = jnp.concatenate([xp_ref[...], x, xn_ref[...]], axis=0)
    h = (_rms(xa, pg_ref[...]) * (1.0 + mod_ref[1:2, :]) + mod_ref[0:1, :]).astype(BF16)
    zab = jnp.dot(h, win_ref[:, :2 * d], preferred_element_type=F32) + bin_ref[:, :2 * d]
    u = zab[:, :d] * jax.nn.sigmoid(zab[:, d:])
    row = lax.broadcasted_iota(jnp.int32, (ts + 2 * halo, 1), 0)
    inside = jnp.logical_and(jnp.logical_or(row >= halo, t > 0),
                             jnp.logical_or(row < ts + halo, t < n_tiles - 1))
    u_ref[...] = jnp.where(inside, u, 0.0)

    off = halo - CONV_KERNEL // 2
    for r0 in range(0, ts, CONV_ROW_CHUNK):
        acc = jnp.broadcast_to(dwb_ref[...], (CONV_ROW_CHUNK, d))
        for tap in range(CONV_KERNEL):
            acc = acc + u_ref[pl.ds(r0 + off + tap, CONV_ROW_CHUNK), :] * dww_ref[tap:tap + 1, :]
        c_ref[pl.ds(r0, CONV_ROW_CHUNK), :] = acc

    cv = c_ref[...]
    mu = jnp.mean(cv, axis=-1, keepdims=True)
    var = jnp.mean(jnp.square(cv - mu), axis=-1, keepdims=True)
    ln = (cv - mu) * lax.rsqrt(var + EPS) * lng_ref[...] + lnb_ref[...]
    g = jnp.dot(h[halo:halo + ts], win_ref[:, 2 * d:], preferred_element_type=F32) + bin_ref[:, 2 * d:]
    m = (_silu(ln) * _silu(g)).astype(BF16)
    y = jnp.dot(m, wout_ref[...], preferred_element_type=F32)
    o_ref[...] = x + mod_ref[2:3, :] * _rms(y, postg_ref[...])


def _odd_layer(x, mod, pre_g, post_g, w_in, b_in, dw_w, dw_b, ln_g, ln_b, w_out):
    b, s, d = x.shape
    ts = TOKEN_TILE
    nt = s // ts
    hb = ts // CONV_HALO
    n_hb = s // CONV_HALO
    row = lambda v: v.reshape(1, -1)
    return pl.pallas_call(
        functools.partial(_odd_kernel, n_tiles=nt),
        grid=(b, nt),
        in_specs=[pl.BlockSpec((None, CONV_HALO, d), lambda i, t: (i, jnp.maximum(t * hb - 1, 0), 0)),
                  pl.BlockSpec((None, ts, d), lambda i, t: (i, t, 0)),
                  pl.BlockSpec((None, CONV_HALO, d), lambda i, t: (i, jnp.minimum((t + 1) * hb, n_hb - 1), 0)),
                  pl.BlockSpec((None, 3, d), lambda i, t: (i, 0, 0)),
                  _const_spec((1, d)), _const_spec(w_in.shape), _const_spec((1, 3 * d)),
                  _const_spec(dw_w.shape), _const_spec((1, d)), _const_spec((1, d)), _const_spec((1, d)),
                  _const_spec(w_out.shape), _const_spec((1, d))],
        out_specs=pl.BlockSpec((None, ts, d), lambda i, t: (i, t, 0)),
        out_shape=jax.ShapeDtypeStruct((b, s, d), F32),
        scratch_shapes=[pltpu.VMEM((ts + 2 * CONV_HALO, d), F32), pltpu.VMEM((ts, d), F32)],
        compiler_params=_cparams(("arbitrary", "arbitrary")),
        name="odd_layer",
    )(x, x, x, mod, row(pre_g), w_in, row(b_in), dw_w, row(dw_b), row(ln_g), row(ln_b), w_out, row(post_g))


def _pack_even_weights(w_in, w_uq, w_ukv, w_out):
    d = w_in.shape[0]
    hq = RET_HEADS * RET_QK_DIM
    sizes = (hq, hq, RET_WIDTH, RET_WIDTH, MLA_Q_RANK, MLA_KV_RANK, MLA_ROPE_DIM, MLA_WIDTH)
    pieces, start = [], 0
    for size in sizes:
        pieces.append(w_in[:, start:start + size])
        start += size
    pieces.insert(7, jnp.zeros((d, LANES - MLA_ROPE_DIM), w_in.dtype))
    w_in_p = jnp.concatenate(pieces, axis=1).astype(BF16)
    wq = w_uq.reshape(MLA_Q_RANK, MLA_HEADS, MLA_NOPE_DIM + MLA_ROPE_DIM)
    w_uq_p = jnp.concatenate([wq[:, :, :MLA_NOPE_DIM].reshape(MLA_Q_RANK, -1),
                              wq[:, :, MLA_NOPE_DIM:].reshape(MLA_Q_RANK, -1)], axis=1).astype(BF16)
    return w_in_p, w_uq_p, w_ukv.astype(BF16), w_out.astype(BF16)


def kernel(x, c, positions, ada_w, ada_b, pre_g, post_g, ev_w_in, ev_dec_f, ev_dec_b, ev_q_norm_g, ev_w_uq, ev_kv_norm_g, ev_w_ukv, ev_w_out, od_w_in, od_b_in, od_dw_w, od_dw_b, od_ln_g, od_ln_b, od_w_out):
    depth = ada_w.shape[0]
    s = x.shape[1]
    assert s % TOKEN_TILE == 0 and TOKEN_TILE % ATT_Q_TILE == 0 and TOKEN_TILE % RET_CHUNK == 0
    mod = _modulation(c, ada_w, ada_b)
    cos_t, sin_t = _rope_tables(positions)
    for layer in range(depth):
        i = layer // 2
        if layer % 2 == 0:
            w_in_p, w_uq_p, w_ukv, w_out = _pack_even_weights(ev_w_in[i], ev_w_uq[i], ev_w_ukv[i], ev_w_out[i])
            rq, rk, rv, gates, qt, kc, vt = _even_in(x, mod[layer], pre_g[layer], w_in_p, ev_q_norm_g[i], w_uq_p,
                                                     ev_kv_norm_g[i], w_ukv, cos_t, sin_t)
            ret = _retention(rq, rk, rv, ev_dec_f[i], ev_dec_b[i])
            att = _attention(qt, kc, vt)
            x = _even_out(x, ret, att, gates, w_out, post_g[layer], mod[layer])
        else:
            x = _odd_layer(x, mod[layer], pre_g[layer], post_g[layer], od_w_in[i].astype(BF16), od_b_in[i],
                           od_dw_w[i], od_dw_b[i], od_ln_g[i], od_ln_b[i], od_w_out[i].astype(BF16))
    return x
```

```python
import functools
import math

import jax
import jax.numpy as jnp
from jax import lax
from jax.experimental import pallas as pl
from jax.experimental.pallas import tpu as pltpu

F32 = jnp.float32
BF16 = jnp.bfloat16

RET_HEADS = 4
RET_QK_DIM = 64
RET_V_DIM = 128
RET_CHUNK = 128
MLA_HEADS = 4
MLA_Q_RANK = 384
MLA_KV_RANK = 256
MLA_NOPE_DIM = 128
MLA_ROPE_DIM = 64
MLA_V_DIM = 128
ROPE_DIM = 64
ROPE_BASE = 10000.0
CONV_KERNEL = 31
EPS = 1e-6

RET_WIDTH = RET_HEADS * RET_V_DIM
MLA_WIDTH = MLA_HEADS * MLA_V_DIM
RET_SCALE = RET_QK_DIM ** -0.5
ATT_SCALE_LOG2 = (MLA_NOPE_DIM + MLA_ROPE_DIM) ** -0.5 * math.log2(math.e)

LANES = 128
MXU_DIM = 256
ATT_QK_PAD = MXU_DIM
CONV_HALO = 16

TOKEN_TILE = 512
ATT_Q_TILE = 512
ATT_KV_TILE = 2 * TOKEN_TILE
ATT_PV_CHUNK = MXU_DIM
ATT_VT_ROWS = MLA_V_DIM + 16
CONV_ROW_CHUNK = 32

V7X_VMEM_BYTES = 64 * 1024 * 1024
VMEM_LIMIT_BYTES = V7X_VMEM_BYTES - 8 * 1024 * 1024

_C_RQ = 0
_C_RK = _C_RQ + RET_HEADS * RET_QK_DIM
_C_RV = _C_RK + RET_HEADS * RET_QK_DIM
_C_RG = _C_RV + RET_WIDTH
_C_CQ = _C_RG + RET_WIDTH
_C_CKV = _C_CQ + MLA_Q_RANK
_C_KR = _C_CKV + MLA_KV_RANK
_C_MG = _C_KR + LANES
_C_END = _C_MG + MLA_WIDTH


def _cparams(semantics):
    return pltpu.CompilerParams(dimension_semantics=semantics, vmem_limit_bytes=VMEM_LIMIT_BYTES)


def _const_spec(shape):
    nd = len(shape)
    return pl.BlockSpec(shape, lambda *_: (0,) * nd, pipeline_mode=pl.Buffered(1))


def _silu(v):
    return v * jax.nn.sigmoid(v)


def _rms(v, g):
    return v * lax.rsqrt(jnp.mean(v * v, axis=-1, keepdims=True) + EPS) * g


def _mod_kernel(c_ref, w_ref, b_ref, o_ref):
    c = c_ref[...]
    o_ref[...] = jnp.dot(_silu(c).astype(BF16), w_ref[...].astype(BF16),
                         preferred_element_type=F32) + b_ref[...]


def _modulation(c, ada_w, ada_b):
    depth, d, d3 = ada_w.shape
    b = c.shape[0]
    rows = -(-b // 8) * 8
    c_pad = jnp.pad(c, ((0, rows - b), (0, 0)))
    out = pl.pallas_call(
        _mod_kernel,
        grid=(depth, d3 // d),
        in_specs=[pl.BlockSpec((rows, d), lambda l, j: (0, 0)),
                  pl.BlockSpec((None, d, d), lambda l, j: (l, 0, j)),
                  pl.BlockSpec((None, 1, d), lambda l, j: (l, 0, j))],
        out_specs=pl.BlockSpec((None, rows, d), lambda l, j: (l, 0, j)),
        out_shape=jax.ShapeDtypeStruct((depth, rows, d3), F32),
        compiler_params=_cparams(("arbitrary", "arbitrary")),
        name="adaln_mod",
    )(c_pad, ada_w, ada_b.reshape(depth, 1, d3))
    return out[:, :b].reshape(depth, b, d3 // d, d)


def _rope_table_kernel(pos_ref, invf_ref, sgn_ref, cos_ref, sin_ref):
    ang = pos_ref[...].astype(F32) * invf_ref[...]
    cos_ref[...] = jnp.cos(ang)
    sin_ref[...] = jnp.sin(ang) * sgn_ref[...]


def _rope_tables(positions):
    b, s = positions.shape
    half = ROPE_DIM // 2
    inv_freq = ROPE_BASE ** (-jnp.arange(0, ROPE_DIM, 2, dtype=F32) / ROPE_DIM)
    invf = jnp.tile(inv_freq, LANES // half).reshape(1, LANES)
    sgn = jnp.tile(jnp.concatenate([-jnp.ones((half,), F32), jnp.ones((half,), F32)]),
                   LANES // ROPE_DIM).reshape(1, LANES)
    ts = min(TOKEN_TILE, s)
    return pl.pallas_call(
        _rope_table_kernel,
        grid=(b, s // ts),
        in_specs=[pl.BlockSpec((None, ts, 1), lambda i, t: (i, t, 0)),
                  pl.BlockSpec((1, LANES), lambda i, t: (0, 0)),
                  pl.BlockSpec((1, LANES), lambda i, t: (0, 0))],
        out_specs=[pl.BlockSpec((None, ts, LANES), lambda i, t: (i, t, 0))] * 2,
        out_shape=[jax.ShapeDtypeStruct((b, s, LANES), F32)] * 2,
        compiler_params=_cparams(("arbitrary", "arbitrary")),
        name="rope_tables",
    )(positions.reshape(b, s, 1), invf, sgn)


def _rope_slab(v, cos, sin_signed, first_half):
    half = ROPE_DIM // 2
    partner = jnp.where(first_half, pltpu.roll(v, LANES - half, 1), pltpu.roll(v, half, 1))
    return v * cos + partner * sin_signed


def _even_in_kernel(x_ref, mod_ref, pg_ref, win_ref, qg_ref, wuq_ref, kvg_ref, wukv_ref,
                    cos_ref, sin_ref,
                    rq_ref, rk_ref, rv_ref, gate_ref, qt_ref, kc_ref, vt_ref):
    x = x_ref[...]
    h = _rms(x, pg_ref[...]) * (1.0 + mod_ref[1:2, :]) + mod_ref[0:1, :]
    z = jnp.dot(h.astype(BF16), win_ref[...], preferred_element_type=F32)

    cos = cos_ref[...]
    sin = sin_ref[...]
    lane = lax.broadcasted_iota(jnp.int32, cos.shape, 1)
    first_half = (lane % ROPE_DIM) < (ROPE_DIM // 2)
    rope = functools.partial(_rope_slab, cos=cos, sin_signed=sin, first_half=first_half)

    for j in range(RET_HEADS * RET_QK_DIM // LANES):
        sl = slice(j * LANES, (j + 1) * LANES)
        rq_ref[:, sl] = rope(z[:, _C_RQ + j * LANES:_C_RQ + (j + 1) * LANES]).astype(BF16)
        rk_ref[:, sl] = (rope(z[:, _C_RK + j * LANES:_C_RK + (j + 1) * LANES]) * RET_SCALE).astype(BF16)
    rv_ref[...] = z[:, _C_RV:_C_RG].astype(BF16)
    gate_ref[:, :RET_WIDTH] = _silu(z[:, _C_RG:_C_CQ]).astype(BF16)
    gate_ref[:, RET_WIDTH:] = _silu(z[:, _C_MG:_C_END]).astype(BF16)

    cqn = _rms(z[:, _C_CQ:_C_CKV], qg_ref[...])
    q = jnp.dot(cqn.astype(BF16), wuq_ref[...], preferred_element_type=F32)
    kvn = _rms(z[:, _C_CKV:_C_KR], kvg_ref[...])
    kv = jnp.dot(kvn.astype(BF16), wukv_ref[...], preferred_element_type=F32)

    k_rope = rope(z[:, _C_KR:_C_MG]).astype(BF16)
    nope_w = MLA_HEADS * MLA_NOPE_DIM
    q_rope = [rope(q[:, nope_w + j * LANES:nope_w + (j + 1) * LANES]) for j in range(2)]
    low = lane < ROPE_DIM
    ones_rows = jnp.ones((ATT_VT_ROWS - MLA_V_DIM, x.shape[0]), BF16)
    for hd in range(MLA_HEADS):
        qn = q[:, hd * MLA_NOPE_DIM:(hd + 1) * MLA_NOPE_DIM] * ATT_SCALE_LOG2
        slab = q_rope[hd // 2]
        if hd % 2:
            slab = pltpu.roll(slab, ROPE_DIM, 1)
        qr = jnp.where(low, slab, 0.0) * ATT_SCALE_LOG2
        qt_ref[hd, :MLA_NOPE_DIM, :] = qn.T.astype(BF16)
        qt_ref[hd, MLA_NOPE_DIM:, :] = qr.T.astype(BF16)
        base = hd * (MLA_NOPE_DIM + MLA_V_DIM)
        kc_ref[hd, :, :MLA_NOPE_DIM] = kv[:, base:base + MLA_NOPE_DIM].astype(BF16)
        kc_ref[hd, :, MLA_NOPE_DIM:] = k_rope
        vt_ref[hd, 0, :MLA_V_DIM, :] = kv[:, base + MLA_NOPE_DIM:base + MLA_NOPE_DIM + MLA_V_DIM].T.astype(BF16)
        vt_ref[hd, 0, MLA_V_DIM:, :] = ones_rows


def _even_in(x, mod, pre_g, w_in_p, q_norm_g, w_uq_p, kv_norm_g, w_ukv, cos_t, sin_t):
    b, s, d = x.shape
    ts = TOKEN_TILE
    nt = s // ts
    hq = RET_HEADS * RET_QK_DIM
    tile3 = lambda w: pl.BlockSpec((None, ts, w), lambda i, t: (i, t, 0))
    out_shapes = [
        jax.ShapeDtypeStruct((b, s, hq), BF16),
        jax.ShapeDtypeStruct((b, s, hq), BF16),
        jax.ShapeDtypeStruct((b, s, RET_WIDTH), BF16),
        jax.ShapeDtypeStruct((b, s, RET_WIDTH + MLA_WIDTH), BF16),
        jax.ShapeDtypeStruct((b, MLA_HEADS, ATT_QK_PAD, s), BF16),
        jax.ShapeDtypeStruct((b, MLA_HEADS, s, ATT_QK_PAD), BF16),
        jax.ShapeDtypeStruct((b, MLA_HEADS, nt, ATT_VT_ROWS, ts), BF16),
    ]
    out_specs = [
        tile3(hq), tile3(hq), tile3(RET_WIDTH), tile3(RET_WIDTH + MLA_WIDTH),
        pl.BlockSpec((None, MLA_HEADS, ATT_QK_PAD, ts), lambda i, t: (i, 0, 0, t)),
        pl.BlockSpec((None, MLA_HEADS, ts, ATT_QK_PAD), lambda i, t: (i, 0, t, 0)),
        pl.BlockSpec((None, MLA_HEADS, 1, ATT_VT_ROWS, ts), lambda i, t: (i, 0, t, 0, 0)),
    ]
    return pl.pallas_call(
        _even_in_kernel,
        grid=(b, nt),
        in_specs=[tile3(d),
                  pl.BlockSpec((None, 3, d), lambda i, t: (i, 0, 0)),
                  _const_spec((1, d)),
                  _const_spec(w_in_p.shape),
                  _const_spec((1, MLA_Q_RANK)),
                  _const_spec(w_uq_p.shape),
                  _const_spec((1, MLA_KV_RANK)),
                  _const_spec(w_ukv.shape),
                  tile3(LANES), tile3(LANES)],
        out_specs=out_specs,
        out_shape=out_shapes,
        compiler_params=_cparams(("arbitrary", "arbitrary")),
        name="even_in",
    )(x, mod, pre_g.reshape(1, d), w_in_p, q_norm_g.reshape(1, -1), w_uq_p,
      kv_norm_g.reshape(1, -1), w_ukv, cos_t, sin_t)


def _retention_kernel(dec_ref, q_ref, k_ref, v_ref, o_ref, sf_ref, kvb_ref, *, n_chunks):
    c = RET_CHUNK
    dk = RET_QK_DIM
    lg = jax.nn.log_sigmoid(dec_ref[...])
    ri = lax.broadcasted_iota(jnp.int32, (c, LANES), 0)
    ci = lax.broadcasted_iota(jnp.int32, (c, LANES), 1)
    rif = ri.astype(F32)
    diff = rif - ci.astype(F32)
    row_h0 = ri < dk
    lane_h0 = ci < dk

    def dmat(lgf, lgb):
        return jnp.where(diff >= 0, jnp.exp(lgf * jnp.maximum(diff, 0.0)), jnp.exp(lgb * jnp.maximum(-diff, 0.0)))

    dec = jnp.concatenate([dmat(lg[0:1], lg[2:3]), dmat(lg[1:2], lg[3:4])], axis=0)
    qwf = jnp.concatenate([jnp.exp(lg[0:1] * (rif + 1.0)), jnp.exp(lg[1:2] * (rif + 1.0))], axis=0)
    qwb = jnp.concatenate([jnp.exp(lg[2:3] * (c - rif)), jnp.exp(lg[3:4] * (c - rif))], axis=0)
    lgf_lane = jnp.where(lane_h0, lg[0:1], lg[1:2])
    lgb_lane = jnp.where(lane_h0, lg[2:3], lg[3:4])
    kwf = jnp.exp(lgf_lane * (c - 1.0 - rif))
    kwb = jnp.exp(lgb_lane * rif)
    cdf = jnp.where(row_h0, jnp.exp(lg[0:1] * c), jnp.exp(lg[1:2] * c))
    cdb = jnp.where(row_h0, jnp.exp(lg[2:3] * c), jnp.exp(lg[3:4] * c))

    def rows_of(n):
        return pl.ds(pl.multiple_of(n * c, c), c)

    def pick_heads(m):
        return jnp.where(row_h0, m[:, :RET_V_DIM], m[:, RET_V_DIM:])

    def fwd_body(n, state_f):
        rows = rows_of(n)
        k = k_ref[rows, :].astype(F32)
        kw_t = jnp.concatenate([k * kwf, k * kwb], axis=1).T.astype(BF16)
        kv = jnp.dot(kw_t, v_ref[rows, :], preferred_element_type=F32)
        sf_ref[n] = state_f.astype(BF16)
        kvb_ref[n] = pick_heads(kv[c:, :])
        return state_f * cdf + pick_heads(kv[:c, :])

    lax.fori_loop(0, n_chunks, fwd_body, jnp.zeros((c, LANES), F32))

    def norm(o):
        mu = jnp.mean(o, axis=-1, keepdims=True)
        var = jnp.mean(jnp.square(o - mu), axis=-1, keepdims=True)
        return (o - mu) * lax.rsqrt(var + EPS)

    def bwd_body(t, state_b):
        n = n_chunks - 1 - t
        rows = rows_of(n)
        q = q_ref[rows, :].astype(F32)
        qm = jnp.concatenate([jnp.where(lane_h0, q, 0.0), jnp.where(lane_h0, 0.0, q)], axis=0)
        s = lax.dot_general(qm.astype(BF16), k_ref[rows, :], (((1,), (1,)), ((), ())),
                            preferred_element_type=F32)
        p = (s * dec).astype(BF16)
        v = v_ref[rows, :]
        inner0 = jnp.dot(p[:c], v[:, :RET_V_DIM], preferred_element_type=F32)
        inner1 = jnp.dot(p[c:], v[:, RET_V_DIM:], preferred_element_type=F32)
        lhs = jnp.concatenate([qm * qwf, qm * qwb], axis=1).astype(BF16)
        rhs = jnp.concatenate([sf_ref[n], state_b.astype(BF16)], axis=0)
        cross = jnp.dot(lhs, rhs, preferred_element_type=F32)
        o_ref[rows, :RET_V_DIM] = norm(inner0 + cross[:c]).astype(BF16)
        o_ref[rows, RET_V_DIM:] = norm(inner1 + cross[c:]).astype(BF16)
        return state_b * cdb + kvb_ref[n]

    lax.fori_loop(0, n_chunks, bwd_body, jnp.zeros((c, LANES), F32))


def _retention(rq, rk, rv, dec_f, dec_b):
    b, s, _ = rq.shape
    n_chunks = s // RET_CHUNK
    pairs = RET_HEADS // 2
    dec = jnp.concatenate([dec_f.reshape(pairs, 2), dec_b.reshape(pairs, 2)], axis=1)
    dec = jnp.broadcast_to(dec[:, :, None], (pairs, 4, LANES)).astype(F32)
    seq = lambda w: pl.BlockSpec((None, s, w), lambda i, p: (i, 0, p))
    return pl.pallas_call(
        functools.partial(_retention_kernel, n_chunks=n_chunks),
        grid=(b, pairs),
        in_specs=[pl.BlockSpec((None, 4, LANES), lambda i, p: (p, 0, 0)),
                  seq(2 * RET_QK_DIM), seq(2 * RET_QK_DIM), seq(2 * RET_V_DIM)],
        out_specs=seq(2 * RET_V_DIM),
        out_shape=jax.ShapeDtypeStruct((b, s, RET_WIDTH), BF16),
        scratch_shapes=[pltpu.VMEM((n_chunks, RET_CHUNK, LANES), BF16),
                        pltpu.VMEM((n_chunks, RET_CHUNK, LANES), F32)],
        compiler_params=_cparams(("arbitrary", "arbitrary")),
        name="retention",
    )(dec, rq, rk, rv)


def _attention_kernel(qt_ref, kc_ref, vt_ref, o_ref, s_ref, *, n_kv, tk):
    qt = qt_ref[...]
    tq = qt.shape[1]
    ck = ATT_PV_CHUNK
    vt_per_kv = tk // TOKEN_TILE
    chunks_per_vt = TOKEN_TILE // ck

    def scores(j):
        k = kc_ref[pl.ds(pl.multiple_of(j * tk, tk), tk), :]
        return jnp.dot(k, qt, preferred_element_type=F32)

    def softmax_pv(slot, j, m, acc):
        m_new = jnp.maximum(m, jnp.max(s_ref[slot], axis=0, keepdims=True))
        acc = jnp.exp2(m - m_new) * acc
        for c in range(tk // ck):
            p = jnp.exp2(s_ref[slot, c * ck:(c + 1) * ck, :] - m_new).astype(BF16)
            lo = (c % chunks_per_vt) * ck
            vt = vt_ref[j * vt_per_kv + c // chunks_per_vt, :, lo:lo + ck]
            acc = acc + jnp.dot(vt, p, preferred_element_type=F32)
        return m_new, acc

    def pair(j, m, acc, last):
        s_ref[1] = scores(j + 1)
        m, acc = softmax_pv(0, j, m, acc)
        if not last:
            s_ref[0] = scores(j + 2)
        return softmax_pv(1, j + 1, m, acc)

    s_ref[0] = scores(0)
    carry = (jnp.full((1, tq), -1e30, F32), jnp.zeros((ATT_VT_ROWS, tq), F32))
    carry = lax.fori_loop(0, n_kv // 2 - 1, lambda i, c: pair(2 * i, c[0], c[1], False), carry)
    _, acc = pair(n_kv - 2, carry[0], carry[1], True)
    o_ref[...] = (acc[:MLA_V_DIM] / acc[MLA_V_DIM:MLA_V_DIM + 1]).T.astype(BF16)


def _attention(qt, kc, vt):
    b, h, _, s = qt.shape
    n_vt = vt.shape[2]
    tq = min(ATT_Q_TILE, s)
    tk = ATT_KV_TILE if s % (2 * ATT_KV_TILE) == 0 else TOKEN_TILE
    assert s % (2 * tk) == 0 and s % tq == 0
    return pl.pallas_call(
        functools.partial(_attention_kernel, n_kv=s // tk, tk=tk),
        grid=(b, h, s // tq),
        in_specs=[pl.BlockSpec((None, None, ATT_QK_PAD, tq), lambda i, j, t: (i, j, 0, t)),
                  pl.BlockSpec((None, None, s, ATT_QK_PAD), lambda i, j, t: (i, j, 0, 0)),
                  pl.BlockSpec((None, None, n_vt, ATT_VT_ROWS, TOKEN_TILE), lambda i, j, t: (i, j, 0, 0, 0))],
        out_specs=pl.BlockSpec((None, tq, MLA_V_DIM), lambda i, j, t: (i, t, j)),
        out_shape=jax.ShapeDtypeStruct((b, s, MLA_WIDTH), BF16),
        scratch_shapes=[pltpu.VMEM((2, tk, tq), F32)],
        compiler_params=_cparams(("arbitrary", "arbitrary", "arbitrary")),
        name="mla_attention",
    )(qt, kc, vt)


def _even_out_kernel(x_ref, ret_ref, att_ref, gate_ref, w_ref, pg_ref, mod_ref, o_ref):
    m_ret = ret_ref[...] * gate_ref[:, :RET_WIDTH]
    m_att = att_ref[...] * gate_ref[:, RET_WIDTH:]
    y = (jnp.dot(m_ret, w_ref[:RET_WIDTH, :], preferred_element_type=F32)
         + jnp.dot(m_att, w_ref[RET_WIDTH:, :], preferred_element_type=F32))
    o_ref[...] = x_ref[...] + mod_ref[2:3, :] * _rms(y, pg_ref[...])


def _even_out(x, ret, att, gates, w_out, post_g, mod):
    b, s, d = x.shape
    ts = TOKEN_TILE
    tile3 = lambda w: pl.BlockSpec((None, ts, w), lambda i, t: (i, t, 0))
    return pl.pallas_call(
        _even_out_kernel,
        grid=(b, s // ts),
        in_specs=[tile3(d), tile3(RET_WIDTH), tile3(MLA_WIDTH), tile3(RET_WIDTH + MLA_WIDTH),
                  _const_spec(w_out.shape), _const_spec((1, d)),
                  pl.BlockSpec((None, 3, d), lambda i, t: (i, 0, 0))],
        out_specs=tile3(d),
        out_shape=jax.ShapeDtypeStruct((b, s, d), F32),
        compiler_params=_cparams(("arbitrary", "arbitrary")),
        name="even_out",
    )(x, ret, att, gates, w_out, post_g.reshape(1, d), mod)


def _odd_kernel(xp_ref, x_ref, xn_ref, mod_ref, pg_ref, win_ref, bin_ref, dww_ref, dwb_ref,
                lng_ref, lnb_ref, wout_ref, postg_ref, o_ref, u_ref, c_ref, *, n_tiles):
    t = pl.program_id(1)
    ts, d = x_ref.shape
    halo = CONV_HALO
    x = x_ref[...]
    xa = jnp.concatenate([xp_ref[...], x, xn_ref[...]], axis=0)
    h = (_rms(xa, pg_ref[...]) * (1.0 + mod_ref[1:2, :]) + mod_ref[0:1, :]).astype(BF16)
    zab = jnp.dot(h, win_ref[:, :2 * d], preferred_element_type=F32) + bin_ref[:, :2 * d]
    u = zab[:, :d] * jax.nn.sigmoid(zab[:, d:])
    row = lax.broadcasted_iota(jnp.int32, (ts + 2 * halo, 1), 0)
    inside = jnp.logical_and(jnp.logical_or(row >= halo, t > 0),
                             jnp.logical_or(row < ts + halo, t < n_tiles - 1))
    u_ref[...] = jnp.where(inside, u, 0.0)

    off = halo - CONV_KERNEL // 2
    for r0 in range(0, ts, CONV_ROW_CHUNK):
        acc = jnp.broadcast_to(dwb_ref[...], (CONV_ROW_CHUNK, d))
        for tap in range(CONV_KERNEL):
            acc = acc + u_ref[pl.ds(r0 + off + tap, CONV_ROW_CHUNK), :] * dww_ref[tap:tap + 1, :]
        c_ref[pl.ds(r0, CONV_ROW_CHUNK), :] = acc

    cv = c_ref[...]
    mu = jnp.mean(cv, axis=-1, keepdims=True)
    var = jnp.mean(jnp.square(cv - mu), axis=-1, keepdims=True)
    ln = (cv - mu) * lax.rsqrt(var + EPS) * lng_ref[...] + lnb_ref[...]
    g = jnp.dot(h[halo:halo + ts], win_ref[:, 2 * d:], preferred_element_type=F32) + bin_ref[:, 2 * d:]
    m = (_silu(ln) * _silu(g)).astype(BF16)
    y = jnp.dot(m, wout_ref[...], preferred_element_type=F32)
    o_ref[...] = x + mod_ref[2:3, :] * _rms(y, postg_ref[...])


def _odd_layer(x, mod, pre_g, post_g, w_in, b_in, dw_w, dw_b, ln_g, ln_b, w_out):
    b, s, d = x.shape
    ts = TOKEN_TILE
    nt = s // ts
    hb = ts // CONV_HALO
    n_hb = s // CONV_HALO
    row = lambda v: v.reshape(1, -1)
    return pl.pallas_call(
        functools.partial(_odd_kernel, n_tiles=nt),
        grid=(b, nt),
        in_specs=[pl.BlockSpec((None, CONV_HALO, d), lambda i, t: (i, jnp.maximum(t * hb - 1, 0), 0)),
                  pl.BlockSpec((None, ts, d), lambda i, t: (i, t, 0)),
                  pl.BlockSpec((None, CONV_HALO, d), lambda i, t: (i, jnp.minimum((t + 1) * hb, n_hb - 1), 0)),
                  pl.BlockSpec((None, 3, d), lambda i, t: (i, 0, 0)),
                  _const_spec((1, d)), _const_spec(w_in.shape), _const_spec((1, 3 * d)),
                  _const_spec(dw_w.shape), _const_spec((1, d)), _const_spec((1, d)), _const_spec((1, d)),
                  _const_spec(w_out.shape), _const_spec((1, d))],
        out_specs=pl.BlockSpec((None, ts, d), lambda i, t: (i, t, 0)),
        out_shape=jax.ShapeDtypeStruct((b, s, d), F32),
        scratch_shapes=[pltpu.VMEM((ts + 2 * CONV_HALO, d), F32), pltpu.VMEM((ts, d), F32)],
        compiler_params=_cparams(("arbitrary", "arbitrary")),
        name="odd_layer",
    )(x, x, x, mod, row(pre_g), w_in, row(b_in), dw_w, row(dw_b), row(ln_g), row(ln_b), w_out, row(post_g))


def _pack_even_weights(w_in, w_uq, w_ukv, w_out):
    d = w_in.shape[0]
    hq = RET_HEADS * RET_QK_DIM
    sizes = (hq, hq, RET_WIDTH, RET_WIDTH, MLA_Q_RANK, MLA_KV_RANK, MLA_ROPE_DIM, MLA_WIDTH)
    pieces, start = [], 0
    for size in sizes:
        pieces.append(w_in[:, start:start + size])
        start += size
    pieces.insert(7, jnp.zeros((d, LANES - MLA_ROPE_DIM), w_in.dtype))
    w_in_p = jnp.concatenate(pieces, axis=1).astype(BF16)
    wq = w_uq.reshape(MLA_Q_RANK, MLA_HEADS, MLA_NOPE_DIM + MLA_ROPE_DIM)
    w_uq_p = jnp.concatenate([wq[:, :, :MLA_NOPE_DIM].reshape(MLA_Q_RANK, -1),
                              wq[:, :, MLA_NOPE_DIM:].reshape(MLA_Q_RANK, -1)], axis=1).astype(BF16)
    return w_in_p, w_uq_p, w_ukv.astype(BF16), w_out.astype(BF16)


def kernel(x, c, positions, ada_w, ada_b, pre_g, post_g, ev_w_in, ev_dec_f, ev_dec_b, ev_q_norm_g, ev_w_uq, ev_kv_norm_g, ev_w_ukv, ev_w_out, od_w_in, od_b_in, od_dw_w, od_dw_b, od_ln_g, od_ln_b, od_w_out):
    depth = ada_w.shape[0]
    s = x.shape[1]
    assert s % TOKEN_TILE == 0 and TOKEN_TILE % RET_CHUNK == 0
    mod = _modulation(c, ada_w, ada_b)
    cos_t, sin_t = _rope_tables(positions)
    for layer in range(depth):
        i = layer // 2
        if layer % 2 == 0:
            w_in_p, w_uq_p, w_ukv, w_out = _pack_even_weights(ev_w_in[i], ev_w_uq[i], ev_w_ukv[i], ev_w_out[i])
            rq, rk, rv, gates, qt, kc, vt = _even_in(x, mod[layer], pre_g[layer], w_in_p, ev_q_norm_g[i], w_uq_p,
                                                     ev_kv_norm_g[i], w_ukv, cos_t, sin_t)
            ret = _retention(rq, rk, rv, ev_dec_f[i], ev_dec_b[i])
            att = _attention(qt, kc, vt)
            x = _even_out(x, ret, att, gates, w_out, post_g[layer], mod[layer])
        else:
            x = _odd_layer(x, mod[layer], pre_g[layer], post_g[layer], od_w_in[i].astype(BF16), od_b_in[i],
                           od_dw_w[i], od_dw_b[i], od_ln_g[i], od_ln_b[i], od_w_out[i].astype(BF16))
    return x
```

```python
import functools
import math

import jax
import jax.numpy as jnp
from jax import lax
from jax.experimental import pallas as pl
from jax.experimental.pallas import tpu as pltpu

F32 = jnp.float32
BF16 = jnp.bfloat16

RET_HEADS = 4
RET_QK_DIM = 64
RET_V_DIM = 128
RET_CHUNK = 128
MLA_HEADS = 4
MLA_Q_RANK = 384
MLA_KV_RANK = 256
MLA_NOPE_DIM = 128
MLA_ROPE_DIM = 64
MLA_V_DIM = 128
ROPE_DIM = 64
ROPE_BASE = 10000.0
CONV_KERNEL = 31
EPS = 1e-6

RET_WIDTH = RET_HEADS * RET_V_DIM
MLA_WIDTH = MLA_HEADS * MLA_V_DIM
RET_SCALE = RET_QK_DIM ** -0.5
ATT_SCALE_LOG2 = (MLA_NOPE_DIM + MLA_ROPE_DIM) ** -0.5 * math.log2(math.e)

LANES = 128
MXU_DIM = 256
ATT_QK_PAD = MXU_DIM
CONV_HALO = 16

TOKEN_TILE = 512
ATT_Q_TILE = 1024
ATT_KV_TILE = 2 * TOKEN_TILE
ATT_PV_CHUNK = MXU_DIM
ATT_VT_ROWS = MLA_V_DIM + 16
CONV_ROW_CHUNK = 64
RET_CHUNKS_PER_ITER = 16

V7X_VMEM_BYTES = 64 * 1024 * 1024
VMEM_LIMIT_BYTES = V7X_VMEM_BYTES - 8 * 1024 * 1024

_C_RQ = 0
_C_RK = _C_RQ + RET_HEADS * RET_QK_DIM
_C_RV = _C_RK + RET_HEADS * RET_QK_DIM
_C_RG = _C_RV + RET_WIDTH
_C_CQ = _C_RG + RET_WIDTH
_C_CKV = _C_CQ + MLA_Q_RANK
_C_KR = _C_CKV + MLA_KV_RANK
_C_MG = _C_KR + LANES
_C_END = _C_MG + MLA_WIDTH


def _cparams(semantics):
    return pltpu.CompilerParams(dimension_semantics=semantics, vmem_limit_bytes=VMEM_LIMIT_BYTES)


def _const_spec(shape):
    nd = len(shape)
    return pl.BlockSpec(shape, lambda *_: (0,) * nd, pipeline_mode=pl.Buffered(1))


def _silu(v):
    return v * jax.nn.sigmoid(v)


def _rms(v, g):
    return v * lax.rsqrt(jnp.mean(v * v, axis=-1, keepdims=True) + EPS) * g


def _mod_kernel(c_ref, w_ref, b_ref, o_ref):
    c = c_ref[...]
    o_ref[...] = jnp.dot(_silu(c).astype(BF16), w_ref[...].astype(BF16),
                         preferred_element_type=F32) + b_ref[...]


def _modulation(c, ada_w, ada_b):
    depth, d, d3 = ada_w.shape
    b = c.shape[0]
    rows = -(-b // 8) * 8
    c_pad = jnp.pad(c, ((0, rows - b), (0, 0)))
    out = pl.pallas_call(
        _mod_kernel,
        grid=(depth, d3 // d),
        in_specs=[pl.BlockSpec((rows, d), lambda l, j: (0, 0)),
                  pl.BlockSpec((None, d, d), lambda l, j: (l, 0, j)),
                  pl.BlockSpec((None, 1, d), lambda l, j: (l, 0, j))],
        out_specs=pl.BlockSpec((None, rows, d), lambda l, j: (l, 0, j)),
        out_shape=jax.ShapeDtypeStruct((depth, rows, d3), F32),
        compiler_params=_cparams(("arbitrary", "arbitrary")),
        name="adaln_mod",
    )(c_pad, ada_w, ada_b.reshape(depth, 1, d3))
    return out[:, :b].reshape(depth, b, d3 // d, d)


def _rope_table_kernel(pos_ref, invf_ref, sgn_ref, cos_ref, sin_ref):
    ang = pos_ref[...].astype(F32) * invf_ref[...]
    cos_ref[...] = jnp.cos(ang)
    sin_ref[...] = jnp.sin(ang) * sgn_ref[...]


def _rope_tables(positions):
    b, s = positions.shape
    half = ROPE_DIM // 2
    inv_freq = ROPE_BASE ** (-jnp.arange(0, ROPE_DIM, 2, dtype=F32) / ROPE_DIM)
    invf = jnp.tile(inv_freq, LANES // half).reshape(1, LANES)
    sgn = jnp.tile(jnp.concatenate([-jnp.ones((half,), F32), jnp.ones((half,), F32)]),
                   LANES // ROPE_DIM).reshape(1, LANES)
    ts = min(TOKEN_TILE, s)
    return pl.pallas_call(
        _rope_table_kernel,
        grid=(b, s // ts),
        in_specs=[pl.BlockSpec((None, ts, 1), lambda i, t: (i, t, 0)),
                  pl.BlockSpec((1, LANES), lambda i, t: (0, 0)),
                  pl.BlockSpec((1, LANES), lambda i, t: (0, 0))],
        out_specs=[pl.BlockSpec((None, ts, LANES), lambda i, t: (i, t, 0))] * 2,
        out_shape=[jax.ShapeDtypeStruct((b, s, LANES), F32)] * 2,
        compiler_params=_cparams(("arbitrary", "arbitrary")),
        name="rope_tables",
    )(positions.reshape(b, s, 1), invf, sgn)


def _rope_slab(v, cos, sin_signed, first_half):
    half = ROPE_DIM // 2
    partner = jnp.where(first_half, pltpu.roll(v, LANES - half, 1), pltpu.roll(v, half, 1))
    return v * cos + partner * sin_signed


def _even_in_kernel(x_ref, mod_ref, pg_ref, win_ref, qg_ref, wuq_ref, kvg_ref, wukv_ref,
                    cos_ref, sin_ref,
                    rq_ref, rk_ref, rv_ref, gate_ref, qt_ref, kc_ref, vt_ref):
    x = x_ref[...]
    h = _rms(x, pg_ref[...]) * (1.0 + mod_ref[1:2, :]) + mod_ref[0:1, :]
    z = jnp.dot(h.astype(BF16), win_ref[...], preferred_element_type=F32)

    cos = cos_ref[...]
    sin = sin_ref[...]
    lane = lax.broadcasted_iota(jnp.int32, cos.shape, 1)
    first_half = (lane % ROPE_DIM) < (ROPE_DIM // 2)
    rope = functools.partial(_rope_slab, cos=cos, sin_signed=sin, first_half=first_half)

    for j in range(RET_HEADS * RET_QK_DIM // LANES):
        sl = slice(j * LANES, (j + 1) * LANES)
        rq_ref[:, sl] = rope(z[:, _C_RQ + j * LANES:_C_RQ + (j + 1) * LANES]).astype(BF16)
        rk_ref[:, sl] = (rope(z[:, _C_RK + j * LANES:_C_RK + (j + 1) * LANES]) * RET_SCALE).astype(BF16)
    rv_ref[...] = z[:, _C_RV:_C_RG].astype(BF16)
    gate_ref[:, :RET_WIDTH] = _silu(z[:, _C_RG:_C_CQ]).astype(BF16)
    gate_ref[:, RET_WIDTH:] = _silu(z[:, _C_MG:_C_END]).astype(BF16)

    cqn = _rms(z[:, _C_CQ:_C_CKV], qg_ref[...])
    q = jnp.dot(cqn.astype(BF16), wuq_ref[...], preferred_element_type=F32)
    kvn = _rms(z[:, _C_CKV:_C_KR], kvg_ref[...])
    kv = jnp.dot(kvn.astype(BF16), wukv_ref[...], preferred_element_type=F32)

    k_rope = rope(z[:, _C_KR:_C_MG]).astype(BF16)
    nope_w = MLA_HEADS * MLA_NOPE_DIM
    q_rope = [rope(q[:, nope_w + j * LANES:nope_w + (j + 1) * LANES]) for j in range(2)]
    low = lane < ROPE_DIM
    ones_rows = jnp.ones((ATT_VT_ROWS - MLA_V_DIM, x.shape[0]), BF16)
    for hd in range(MLA_HEADS):
        qn = q[:, hd * MLA_NOPE_DIM:(hd + 1) * MLA_NOPE_DIM] * ATT_SCALE_LOG2
        slab = q_rope[hd // 2]
        if hd % 2:
            slab = pltpu.roll(slab, ROPE_DIM, 1)
        qr = jnp.where(low, slab, 0.0) * ATT_SCALE_LOG2
        qt_ref[hd, :MLA_NOPE_DIM, :] = qn.T.astype(BF16)
        qt_ref[hd, MLA_NOPE_DIM:, :] = qr.T.astype(BF16)
        base = hd * (MLA_NOPE_DIM + MLA_V_DIM)
        kc_ref[hd, :, :MLA_NOPE_DIM] = kv[:, base:base + MLA_NOPE_DIM].astype(BF16)
        kc_ref[hd, :, MLA_NOPE_DIM:] = k_rope
        vt_ref[hd, 0, :MLA_V_DIM, :] = kv[:, base + MLA_NOPE_DIM:base + MLA_NOPE_DIM + MLA_V_DIM].T.astype(BF16)
        vt_ref[hd, 0, MLA_V_DIM:, :] = ones_rows


def _even_in(x, mod, pre_g, w_in_p, q_norm_g, w_uq_p, kv_norm_g, w_ukv, cos_t, sin_t):
    b, s, d = x.shape
    ts = TOKEN_TILE
    nt = s // ts
    hq = RET_HEADS * RET_QK_DIM
    tile3 = lambda w: pl.BlockSpec((None, ts, w), lambda i, t: (i, t, 0))
    out_shapes = [
        jax.ShapeDtypeStruct((b, s, hq), BF16),
        jax.ShapeDtypeStruct((b, s, hq), BF16),
        jax.ShapeDtypeStruct((b, s, RET_WIDTH), BF16),
        jax.ShapeDtypeStruct((b, s, RET_WIDTH + MLA_WIDTH), BF16),
        jax.ShapeDtypeStruct((b, MLA_HEADS, ATT_QK_PAD, s), BF16),
        jax.ShapeDtypeStruct((b, MLA_HEADS, s, ATT_QK_PAD), BF16),
        jax.ShapeDtypeStruct((b, MLA_HEADS, nt, ATT_VT_ROWS, ts), BF16),
    ]
    out_specs = [
        tile3(hq), tile3(hq), tile3(RET_WIDTH), tile3(RET_WIDTH + MLA_WIDTH),
        pl.BlockSpec((None, MLA_HEADS, ATT_QK_PAD, ts), lambda i, t: (i, 0, 0, t)),
        pl.BlockSpec((None, MLA_HEADS, ts, ATT_QK_PAD), lambda i, t: (i, 0, t, 0)),
        pl.BlockSpec((None, MLA_HEADS, 1, ATT_VT_ROWS, ts), lambda i, t: (i, 0, t, 0, 0)),
    ]
    return pl.pallas_call(
        _even_in_kernel,
        grid=(b, nt),
        in_specs=[tile3(d),
                  pl.BlockSpec((None, 3, d), lambda i, t: (i, 0, 0)),
                  _const_spec((1, d)),
                  _const_spec(w_in_p.shape),
                  _const_spec((1, MLA_Q_RANK)),
                  _const_spec(w_uq_p.shape),
                  _const_spec((1, MLA_KV_RANK)),
                  _const_spec(w_ukv.shape),
                  tile3(LANES), tile3(LANES)],
        out_specs=out_specs,
        out_shape=out_shapes,
        compiler_params=_cparams(("arbitrary", "arbitrary")),
        name="even_in",
    )(x, mod, pre_g.reshape(1, d), w_in_p, q_norm_g.reshape(1, -1), w_uq_p,
      kv_norm_g.reshape(1, -1), w_ukv, cos_t, sin_t)


def _retention_kernel(dec_ref, q_ref, k_ref, v_ref, o_ref, sf_ref, kvb_ref, *, n_chunks, per_iter):
    c = RET_CHUNK
    dk = RET_QK_DIM
    lg = jax.nn.log_sigmoid(dec_ref[...])
    ri = lax.broadcasted_iota(jnp.int32, (c, LANES), 0)
    ci = lax.broadcasted_iota(jnp.int32, (c, LANES), 1)
    rif = ri.astype(F32)
    diff = rif - ci.astype(F32)
    row_h0 = ri < dk
    lane_h0 = ci < dk

    def dmat(lgf, lgb):
        return jnp.where(diff >= 0, jnp.exp(lgf * jnp.maximum(diff, 0.0)), jnp.exp(lgb * jnp.maximum(-diff, 0.0)))

    dec = jnp.concatenate([dmat(lg[0:1], lg[2:3]), dmat(lg[1:2], lg[3:4])], axis=0)
    qwf = jnp.concatenate([jnp.exp(lg[0:1] * (rif + 1.0)), jnp.exp(lg[1:2] * (rif + 1.0))], axis=0)
    qwb = jnp.concatenate([jnp.exp(lg[2:3] * (c - rif)), jnp.exp(lg[3:4] * (c - rif))], axis=0)
    lgf_lane = jnp.where(lane_h0, lg[0:1], lg[1:2])
    lgb_lane = jnp.where(lane_h0, lg[2:3], lg[3:4])
    kwf = jnp.exp(lgf_lane * (c - 1.0 - rif))
    kwb = jnp.exp(lgb_lane * rif)
    cdf = jnp.where(row_h0, jnp.exp(lg[0:1] * c), jnp.exp(lg[1:2] * c))
    cdb = jnp.where(row_h0, jnp.exp(lg[2:3] * c), jnp.exp(lg[3:4] * c))

    def rows_of(n):
        return pl.ds(pl.multiple_of(n * c, c), c)

    def pick_heads(m):
        return jnp.where(row_h0, m[:, :RET_V_DIM], m[:, RET_V_DIM:])

    def fwd_body(n, state_f):
        rows = rows_of(n)
        k = k_ref[rows, :].astype(F32)
        kw_t = jnp.concatenate([k * kwf, k * kwb], axis=1).T.astype(BF16)
        kv = jnp.dot(kw_t, v_ref[rows, :], preferred_element_type=F32)
        sf_ref[n] = state_f.astype(BF16)
        kvb_ref[n] = pick_heads(kv[c:, :])
        return state_f * cdf + pick_heads(kv[:c, :])

    def grouped(body):
        def group(i, state):
            for u in range(per_iter):
                state = body(i * per_iter + u, state)
            return state
        return group

    lax.fori_loop(0, n_chunks // per_iter, grouped(fwd_body), jnp.zeros((c, LANES), F32))

    def norm(o):
        mu = jnp.mean(o, axis=-1, keepdims=True)
        var = jnp.mean(jnp.square(o - mu), axis=-1, keepdims=True)
        return (o - mu) * lax.rsqrt(var + EPS)

    def bwd_body(t, state_b):
        n = n_chunks - 1 - t
        rows = rows_of(n)
        q = q_ref[rows, :].astype(F32)
        qm = jnp.concatenate([jnp.where(lane_h0, q, 0.0), jnp.where(lane_h0, 0.0, q)], axis=0)
        s = lax.dot_general(qm.astype(BF16), k_ref[rows, :], (((1,), (1,)), ((), ())),
                            preferred_element_type=F32)
        p = (s * dec).astype(BF16)
        v = v_ref[rows, :]
        inner0 = jnp.dot(p[:c], v[:, :RET_V_DIM], preferred_element_type=F32)
        inner1 = jnp.dot(p[c:], v[:, RET_V_DIM:], preferred_element_type=F32)
        lhs = jnp.concatenate([qm * qwf, qm * qwb], axis=1).astype(BF16)
        rhs = jnp.concatenate([sf_ref[n], state_b.astype(BF16)], axis=0)
        cross = jnp.dot(lhs, rhs, preferred_element_type=F32)
        o_ref[rows, :RET_V_DIM] = norm(inner0 + cross[:c]).astype(BF16)
        o_ref[rows, RET_V_DIM:] = norm(inner1 + cross[c:]).astype(BF16)
        return state_b * cdb + kvb_ref[n]

    lax.fori_loop(0, n_chunks // per_iter, grouped(bwd_body), jnp.zeros((c, LANES), F32))


def _retention(rq, rk, rv, dec_f, dec_b):
    b, s, _ = rq.shape
    n_chunks = s // RET_CHUNK
    pairs = RET_HEADS // 2
    dec = jnp.concatenate([dec_f.reshape(pairs, 2), dec_b.reshape(pairs, 2)], axis=1)
    dec = jnp.broadcast_to(dec[:, :, None], (pairs, 4, LANES)).astype(F32)
    seq = lambda w: pl.BlockSpec((None, s, w), lambda i, p: (i, 0, p))
    return pl.pallas_call(
        functools.partial(_retention_kernel, n_chunks=n_chunks, per_iter=math.gcd(n_chunks, RET_CHUNKS_PER_ITER)),
        grid=(b, pairs),
        in_specs=[pl.BlockSpec((None, 4, LANES), lambda i, p: (p, 0, 0)),
                  seq(2 * RET_QK_DIM), seq(2 * RET_QK_DIM), seq(2 * RET_V_DIM)],
        out_specs=seq(2 * RET_V_DIM),
        out_shape=jax.ShapeDtypeStruct((b, s, RET_WIDTH), BF16),
        scratch_shapes=[pltpu.VMEM((n_chunks, RET_CHUNK, LANES), BF16),
                        pltpu.VMEM((n_chunks, RET_CHUNK, LANES), F32)],
        compiler_params=_cparams(("arbitrary", "arbitrary")),
        name="retention",
    )(dec, rq, rk, rv)


def _attention_kernel(qt_ref, kc_ref, vt_ref, o_ref, s_ref, *, n_kv, tk):
    qt = qt_ref[...]
    tq = qt.shape[1]
    ck = ATT_PV_CHUNK
    vt_per_kv = tk // TOKEN_TILE
    chunks_per_vt = TOKEN_TILE // ck

    def scores(j):
        k = kc_ref[pl.ds(pl.multiple_of(j * tk, tk), tk), :]
        return jnp.dot(k, qt, preferred_element_type=F32)

    def softmax_pv(slot, j, m, acc):
        m_new = jnp.maximum(m, jnp.max(s_ref[slot], axis=0, keepdims=True))
        acc = jnp.exp2(m - m_new) * acc
        for c in range(tk // ck):
            p = jnp.exp2(s_ref[slot, c * ck:(c + 1) * ck, :] - m_new).astype(BF16)
            lo = (c % chunks_per_vt) * ck
            vt = vt_ref[j * vt_per_kv + c // chunks_per_vt, :, lo:lo + ck]
            acc = acc + jnp.dot(vt, p, preferred_element_type=F32)
        return m_new, acc

    def pair(j, m, acc, last):
        s_ref[1] = scores(j + 1)
        m, acc = softmax_pv(0, j, m, acc)
        if not last:
            s_ref[0] = scores(j + 2)
        return softmax_pv(1, j + 1, m, acc)

    s_ref[0] = scores(0)
    carry = (jnp.full((1, tq), -1e30, F32), jnp.zeros((ATT_VT_ROWS, tq), F32))
    carry = lax.fori_loop(0, n_kv // 2 - 1, lambda i, c: pair(2 * i, c[0], c[1], False), carry)
    _, acc = pair(n_kv - 2, carry[0], carry[1], True)
    o_ref[...] = (acc[:MLA_V_DIM] / acc[MLA_V_DIM:MLA_V_DIM + 1]).T.astype(BF16)


def _attention(qt, kc, vt):
    b, h, _, s = qt.shape
    n_vt = vt.shape[2]
    tq = min(ATT_Q_TILE, s)
    tk = ATT_KV_TILE if s % (2 * ATT_KV_TILE) == 0 else TOKEN_TILE
    assert s % (2 * tk) == 0 and s % tq == 0
    return pl.pallas_call(
        functools.partial(_attention_kernel, n_kv=s // tk, tk=tk),
        grid=(b, h, s // tq),
        in_specs=[pl.BlockSpec((None, None, ATT_QK_PAD, tq), lambda i, j, t: (i, j, 0, t)),
                  pl.BlockSpec((None, None, s, ATT_QK_PAD), lambda i, j, t: (i, j, 0, 0)),
                  pl.BlockSpec((None, None, n_vt, ATT_VT_ROWS, TOKEN_TILE), lambda i, j, t: (i, j, 0, 0, 0))],
        out_specs=pl.BlockSpec((None, tq, MLA_V_DIM), lambda i, j, t: (i, t, j)),
        out_shape=jax.ShapeDtypeStruct((b, s, MLA_WIDTH), BF16),
        scratch_shapes=[pltpu.VMEM((2, tk, tq), F32)],
        compiler_params=_cparams(("arbitrary", "arbitrary", "arbitrary")),
        name="mla_attention",
    )(qt, kc, vt)


def _even_out_kernel(x_ref, ret_ref, att_ref, gate_ref, w_ref, pg_ref, mod_ref, o_ref):
    m_ret = ret_ref[...] * gate_ref[:, :RET_WIDTH]
    m_att = att_ref[...] * gate_ref[:, RET_WIDTH:]
    y = (jnp.dot(m_ret, w_ref[:RET_WIDTH, :], preferred_element_type=F32)
         + jnp.dot(m_att, w_ref[RET_WIDTH:, :], preferred_element_type=F32))
    o_ref[...] = x_ref[...] + mod_ref[2:3, :] * _rms(y, pg_ref[...])


def _even_out(x, ret, att, gates, w_out, post_g, mod):
    b, s, d = x.shape
    ts = TOKEN_TILE
    tile3 = lambda w: pl.BlockSpec((None, ts, w), lambda i, t: (i, t, 0))
    return pl.pallas_call(
        _even_out_kernel,
        grid=(b, s // ts),
        in_specs=[tile3(d), tile3(RET_WIDTH), tile3(MLA_WIDTH), tile3(RET_WIDTH + MLA_WIDTH),
                  _const_spec(w_out.shape), _const_spec((1, d)),
                  pl.BlockSpec((None, 3, d), lambda i, t: (i, 0, 0))],
        out_specs=tile3(d),
        out_shape=jax.ShapeDtypeStruct((b, s, d), F32),
        compiler_params=_cparams(("arbitrary", "arbitrary")),
        name="even_out",
    )(x, ret, att, gates, w_out, post_g.reshape(1, d), mod)


def _odd_kernel(xp_ref, x_ref, xn_ref, mod_ref, pg_ref, win_ref, bin_ref, dww_ref, dwb_ref,
                lng_ref, lnb_ref, wout_ref, postg_ref, o_ref, u_ref, c_ref, *, n_tiles):
    t = pl.program_id(1)
    ts, d = x_ref.shape
    halo = CONV_HALO
    x = x_ref[...]
    xa = jnp.concatenate([xp_ref[...], x, xn_ref[...]], axis=0)
    h = (_rms(xa, pg_ref[...]) * (1.0 + mod_ref[1:2, :]) + mod_ref[0:1, :]).astype(BF16)
    zab = jnp.dot(h, win_ref[:, :2 * d], preferred_element_type=F32) + bin_ref[:, :2 * d]
    u = zab[:, :d] * jax.nn.sigmoid(zab[:, d:])
    row = lax.broadcasted_iota(jnp.int32, (ts + 2 * halo, 1), 0)
    inside = jnp.logical_and(jnp.logical_or(row >= halo, t > 0),
                             jnp.logical_or(row < ts + halo, t < n_tiles - 1))
    u = jnp.where(inside, u, 0.0)
    for sl in range(d // LANES):
        u_ref[sl] = u[:, sl * LANES:(sl + 1) * LANES]

    off = halo - CONV_KERNEL // 2
    for sl in range(d // LANES):
        cs = slice(sl * LANES, (sl + 1) * LANES)
        for r0 in range(0, ts, CONV_ROW_CHUNK):
            acc = jnp.broadcast_to(dwb_ref[:, cs], (CONV_ROW_CHUNK, LANES))
            for tap in range(CONV_KERNEL):
                acc = acc + u_ref[sl, pl.ds(r0 + off + tap, CONV_ROW_CHUNK), :] * dww_ref[tap:tap + 1, cs]
            c_ref[pl.ds(r0, CONV_ROW_CHUNK), cs] = acc

    cv = c_ref[...]
    mu = jnp.mean(cv, axis=-1, keepdims=True)
    var = jnp.mean(jnp.square(cv - mu), axis=-1, keepdims=True)
    ln = (cv - mu) * lax.rsqrt(var + EPS) * lng_ref[...] + lnb_ref[...]
    g = jnp.dot(h[halo:halo + ts], win_ref[:, 2 * d:], preferred_element_type=F32) + bin_ref[:, 2 * d:]
    m = (_silu(ln) * _silu(g)).astype(BF16)
    y = jnp.dot(m, wout_ref[...], preferred_element_type=F32)
    o_ref[...] = x + mod_ref[2:3, :] * _rms(y, postg_ref[...])


def _odd_layer(x, mod, pre_g, post_g, w_in, b_in, dw_w, dw_b, ln_g, ln_b, w_out):
    b, s, d = x.shape
    ts = TOKEN_TILE
    nt = s // ts
    hb = ts // CONV_HALO
    n_hb = s // CONV_HALO
    row = lambda v: v.reshape(1, -1)
    return pl.pallas_call(
        functools.partial(_odd_kernel, n_tiles=nt),
        grid=(b, nt),
        in_specs=[pl.BlockSpec((None, CONV_HALO, d), lambda i, t: (i, jnp.maximum(t * hb - 1, 0), 0)),
                  pl.BlockSpec((None, ts, d), lambda i, t: (i, t, 0)),
                  pl.BlockSpec((None, CONV_HALO, d), lambda i, t: (i, jnp.minimum((t + 1) * hb, n_hb - 1), 0)),
                  pl.BlockSpec((None, 3, d), lambda i, t: (i, 0, 0)),
                  _const_spec((1, d)), _const_spec(w_in.shape), _const_spec((1, 3 * d)),
                  _const_spec(dw_w.shape), _const_spec((1, d)), _const_spec((1, d)), _const_spec((1, d)),
                  _const_spec(w_out.shape), _const_spec((1, d))],
        out_specs=pl.BlockSpec((None, ts, d), lambda i, t: (i, t, 0)),
        out_shape=jax.ShapeDtypeStruct((b, s, d), F32),
        scratch_shapes=[pltpu.VMEM((d // LANES, ts + 2 * CONV_HALO, LANES), F32), pltpu.VMEM((ts, d), F32)],
        compiler_params=_cparams(("arbitrary", "arbitrary")),
        name="odd_layer",
    )(x, x, x, mod, row(pre_g), w_in, row(b_in), dw_w, row(dw_b), row(ln_g), row(ln_b), w_out, row(post_g))


def _pack_even_weights(w_in, w_uq, w_ukv, w_out):
    d = w_in.shape[0]
    hq = RET_HEADS * RET_QK_DIM
    sizes = (hq, hq, RET_WIDTH, RET_WIDTH, MLA_Q_RANK, MLA_KV_RANK, MLA_ROPE_DIM, MLA_WIDTH)
    pieces, start = [], 0
    for size in sizes:
        pieces.append(w_in[:, start:start + size])
        start += size
    pieces.insert(7, jnp.zeros((d, LANES - MLA_ROPE_DIM), w_in.dtype))
    w_in_p = jnp.concatenate(pieces, axis=1).astype(BF16)
    wq = w_uq.reshape(MLA_Q_RANK, MLA_HEADS, MLA_NOPE_DIM + MLA_ROPE_DIM)
    w_uq_p = jnp.concatenate([wq[:, :, :MLA_NOPE_DIM].reshape(MLA_Q_RANK, -1),
                              wq[:, :, MLA_NOPE_DIM:].reshape(MLA_Q_RANK, -1)], axis=1).astype(BF16)
    return w_in_p, w_uq_p, w_ukv.astype(BF16), w_out.astype(BF16)


def kernel(x, c, positions, ada_w, ada_b, pre_g, post_g, ev_w_in, ev_dec_f, ev_dec_b, ev_q_norm_g, ev_w_uq, ev_kv_norm_g, ev_w_ukv, ev_w_out, od_w_in, od_b_in, od_dw_w, od_dw_b, od_ln_g, od_ln_b, od_w_out):
    depth = ada_w.shape[0]
    s = x.shape[1]
    assert s % TOKEN_TILE == 0 and TOKEN_TILE % RET_CHUNK == 0
    mod = _modulation(c, ada_w, ada_b)
    cos_t, sin_t = _rope_tables(positions)
    for layer in range(depth):
        i = layer // 2
        if layer % 2 == 0:
            w_in_p, w_uq_p, w_ukv, w_out = _pack_even_weights(ev_w_in[i], ev_w_uq[i], ev_w_ukv[i], ev_w_out[i])
            rq, rk, rv, gates, qt, kc, vt = _even_in(x, mod[layer], pre_g[layer], w_in_p, ev_q_norm_g[i], w_uq_p,
                                                     ev_kv_norm_g[i], w_ukv, cos_t, sin_t)
            ret = _retention(rq, rk, rv, ev_dec_f[i], ev_dec_b[i])
            att = _attention(qt, kc, vt)
            x = _even_out(x, ret, att, gates, w_out, post_g[layer], mod[layer])
        else:
            x = _odd_layer(x, mod[layer], pre_g[layer], post_g[layer], od_w_in[i].astype(BF16), od_b_in[i],
                           od_dw_w[i], od_dw_b[i], od_ln_g[i], od_ln_b[i], od_w_out[i].astype(BF16))
    return x
```

```python
import functools
import math

import jax
import jax.numpy as jnp
from jax import lax
from jax.experimental import pallas as pl
from jax.experimental.pallas import tpu as pltpu

F32 = jnp.float32
BF16 = jnp.bfloat16

RET_HEADS = 4
RET_QK_DIM = 64
RET_V_DIM = 128
RET_CHUNK = 128
MLA_HEADS = 4
MLA_Q_RANK = 384
MLA_KV_RANK = 256
MLA_NOPE_DIM = 128
MLA_ROPE_DIM = 64
MLA_V_DIM = 128
ROPE_DIM = 64
ROPE_BASE = 10000.0
CONV_KERNEL = 31
EPS = 1e-6

RET_WIDTH = RET_HEADS * RET_V_DIM
MLA_WIDTH = MLA_HEADS * MLA_V_DIM
RET_SCALE = RET_QK_DIM ** -0.5
ATT_SCALE_LOG2 = (MLA_NOPE_DIM + MLA_ROPE_DIM) ** -0.5 * math.log2(math.e)

LANES = 128
MXU_DIM = 256
ATT_QK_PAD = MXU_DIM
CONV_HALO = 16

TOKEN_TILE = 512
ATT_Q_TILE = 1024
ATT_KV_TILE = 2 * TOKEN_TILE
ATT_PV_CHUNK = MXU_DIM
ATT_VT_ROWS = MLA_V_DIM + 16
CONV_ROW_CHUNK = 64
ODD_COL_BLOCK = MXU_DIM
RET_CHUNKS_PER_ITER = 16

V7X_VMEM_BYTES = 64 * 1024 * 1024
VMEM_LIMIT_BYTES = V7X_VMEM_BYTES - 8 * 1024 * 1024

_C_RQ = 0
_C_RK = _C_RQ + RET_HEADS * RET_QK_DIM
_C_RV = _C_RK + RET_HEADS * RET_QK_DIM
_C_RG = _C_RV + RET_WIDTH
_C_CQ = _C_RG + RET_WIDTH
_C_CKV = _C_CQ + MLA_Q_RANK
_C_KR = _C_CKV + MLA_KV_RANK
_C_MG = _C_KR + LANES
_C_END = _C_MG + MLA_WIDTH


def _cparams(semantics):
    return pltpu.CompilerParams(dimension_semantics=semantics, vmem_limit_bytes=VMEM_LIMIT_BYTES)


def _const_spec(shape):
    nd = len(shape)
    return pl.BlockSpec(shape, lambda *_: (0,) * nd, pipeline_mode=pl.Buffered(1))


def _silu(v):
    return v * jax.nn.sigmoid(v)


def _rms(v, g):
    return v * lax.rsqrt(jnp.mean(v * v, axis=-1, keepdims=True) + EPS) * g


def _mod_kernel(c_ref, w_ref, b_ref, o_ref):
    c = c_ref[...]
    o_ref[...] = jnp.dot(_silu(c).astype(BF16), w_ref[...].astype(BF16),
                         preferred_element_type=F32) + b_ref[...]


def _modulation(c, ada_w, ada_b):
    depth, d, d3 = ada_w.shape
    b = c.shape[0]
    rows = -(-b // 8) * 8
    c_pad = jnp.pad(c, ((0, rows - b), (0, 0)))
    out = pl.pallas_call(
        _mod_kernel,
        grid=(depth, d3 // d),
        in_specs=[pl.BlockSpec((rows, d), lambda l, j: (0, 0)),
                  pl.BlockSpec((None, d, d), lambda l, j: (l, 0, j)),
                  pl.BlockSpec((None, 1, d), lambda l, j: (l, 0, j))],
        out_specs=pl.BlockSpec((None, rows, d), lambda l, j: (l, 0, j)),
        out_shape=jax.ShapeDtypeStruct((depth, rows, d3), F32),
        compiler_params=_cparams(("arbitrary", "arbitrary")),
        name="adaln_mod",
    )(c_pad, ada_w, ada_b.reshape(depth, 1, d3))
    return out[:, :b].reshape(depth, b, d3 // d, d)


def _rope_table_kernel(pos_ref, invf_ref, cos_ref, sin_ref):
    ang = pos_ref[...].astype(F32) * invf_ref[...]
    cos_ref[...] = jnp.cos(ang)
    sin_ref[...] = jnp.sin(ang)


def _rope_tables(positions):
    b, s = positions.shape
    half = ROPE_DIM // 2
    per_row = LANES // half
    inv_freq = ROPE_BASE ** (-jnp.arange(0, ROPE_DIM, 2, dtype=F32) / ROPE_DIM)
    invf = jnp.tile(inv_freq, per_row).reshape(1, LANES)
    pos = jnp.repeat(positions.reshape(b, s // per_row, per_row), half, axis=-1)
    rows = s // per_row
    tr = min(TOKEN_TILE, rows)
    cos, sin = pl.pallas_call(
        _rope_table_kernel,
        grid=(b, rows // tr),
        in_specs=[pl.BlockSpec((None, tr, LANES), lambda i, t: (i, t, 0)),
                  pl.BlockSpec((1, LANES), lambda i, t: (0, 0))],
        out_specs=[pl.BlockSpec((None, tr, LANES), lambda i, t: (i, t, 0))] * 2,
        out_shape=[jax.ShapeDtypeStruct((b, rows, LANES), F32)] * 2,
        compiler_params=_cparams(("arbitrary", "arbitrary")),
        name="rope_tables",
    )(pos, invf)
    widen = lambda tbl: jnp.tile(tbl.reshape(b, s, half), (1, 1, per_row))
    return widen(cos), widen(sin)


def _rope_slab(v, cos, sin, first_half):
    half = ROPE_DIM // 2
    partner = jnp.where(first_half, -pltpu.roll(v, LANES - half, 1), pltpu.roll(v, half, 1))
    return v * cos + partner * sin


def _even_in_kernel(x_ref, mod_ref, pg_ref, win_ref, qg_ref, wuq_ref, kvg_ref, wukv_ref,
                    cos_ref, sin_ref,
                    rq_ref, rk_ref, rv_ref, gate_ref, qt_ref, kc_ref, vt_ref):
    x = x_ref[...]
    h = (_rms(x, pg_ref[...]) * (1.0 + mod_ref[1:2, :]) + mod_ref[0:1, :]).astype(BF16)

    def proj(lo, hi):
        return jnp.dot(h, win_ref[:, lo:hi], preferred_element_type=F32)

    cos = cos_ref[...]
    sin = sin_ref[...]
    lane = lax.broadcasted_iota(jnp.int32, cos.shape, 1)
    first_half = (lane % ROPE_DIM) < (ROPE_DIM // 2)
    rope = functools.partial(_rope_slab, cos=cos, sin=sin, first_half=first_half)

    z_lat = proj(_C_CQ, _C_MG)
    cqn = _rms(z_lat[:, :MLA_Q_RANK], qg_ref[...])
    q = jnp.dot(cqn.astype(BF16), wuq_ref[...], preferred_element_type=F32)
    kvn = _rms(z_lat[:, MLA_Q_RANK:MLA_Q_RANK + MLA_KV_RANK], kvg_ref[...])
    kv = jnp.dot(kvn.astype(BF16), wukv_ref[...], preferred_element_type=F32)
    k_rope = rope(z_lat[:, _C_KR - _C_CQ:]).astype(BF16)

    z_ret = proj(_C_RQ, _C_CQ)
    for j in range(RET_HEADS * RET_QK_DIM // LANES):
        sl = slice(j * LANES, (j + 1) * LANES)
        rq_ref[:, sl] = rope(z_ret[:, _C_RQ + j * LANES:_C_RQ + (j + 1) * LANES]).astype(BF16)
        rk_ref[:, sl] = (rope(z_ret[:, _C_RK + j * LANES:_C_RK + (j + 1) * LANES]) * RET_SCALE).astype(BF16)
    rv_ref[...] = z_ret[:, _C_RV:_C_RG].astype(BF16)
    gate_ref[:, :RET_WIDTH] = _silu(z_ret[:, _C_RG:_C_CQ]).astype(BF16)
    gate_ref[:, RET_WIDTH:] = _silu(proj(_C_MG, _C_END)).astype(BF16)

    nope_w = MLA_HEADS * MLA_NOPE_DIM
    q_rope = [rope(q[:, nope_w + j * LANES:nope_w + (j + 1) * LANES]) for j in range(2)]
    low = lane < ROPE_DIM
    ones_rows = jnp.ones((ATT_VT_ROWS - MLA_V_DIM, x.shape[0]), BF16)
    for hd in range(MLA_HEADS):
        qn = q[:, hd * MLA_NOPE_DIM:(hd + 1) * MLA_NOPE_DIM] * ATT_SCALE_LOG2
        slab = q_rope[hd // 2]
        if hd % 2:
            slab = pltpu.roll(slab, ROPE_DIM, 1)
        qr = jnp.where(low, slab, 0.0) * ATT_SCALE_LOG2
        qt_ref[hd, :MLA_NOPE_DIM, :] = qn.T.astype(BF16)
        qt_ref[hd, MLA_NOPE_DIM:, :] = qr.T.astype(BF16)
        base = hd * (MLA_NOPE_DIM + MLA_V_DIM)
        kc_ref[hd, :, :MLA_NOPE_DIM] = kv[:, base:base + MLA_NOPE_DIM].astype(BF16)
        kc_ref[hd, :, MLA_NOPE_DIM:] = k_rope
        vt_ref[hd, 0, :MLA_V_DIM, :] = kv[:, base + MLA_NOPE_DIM:base + MLA_NOPE_DIM + MLA_V_DIM].T.astype(BF16)
        vt_ref[hd, 0, MLA_V_DIM:, :] = ones_rows


def _even_in(x, mod, pre_g, w_in_p, q_norm_g, w_uq_p, kv_norm_g, w_ukv, cos_t, sin_t):
    b, s, d = x.shape
    ts = TOKEN_TILE
    nt = s // ts
    hq = RET_HEADS * RET_QK_DIM
    tile3 = lambda w: pl.BlockSpec((None, ts, w), lambda i, t: (i, t, 0))
    out_shapes = [
        jax.ShapeDtypeStruct((b, s, hq), BF16),
        jax.ShapeDtypeStruct((b, s, hq), BF16),
        jax.ShapeDtypeStruct((b, s, RET_WIDTH), BF16),
        jax.ShapeDtypeStruct((b, s, RET_WIDTH + MLA_WIDTH), BF16),
        jax.ShapeDtypeStruct((b, MLA_HEADS, ATT_QK_PAD, s), BF16),
        jax.ShapeDtypeStruct((b, MLA_HEADS, s, ATT_QK_PAD), BF16),
        jax.ShapeDtypeStruct((b, MLA_HEADS, nt, ATT_VT_ROWS, ts), BF16),
    ]
    out_specs = [
        tile3(hq), tile3(hq), tile3(RET_WIDTH), tile3(RET_WIDTH + MLA_WIDTH),
        pl.BlockSpec((None, MLA_HEADS, ATT_QK_PAD, ts), lambda i, t: (i, 0, 0, t)),
        pl.BlockSpec((None, MLA_HEADS, ts, ATT_QK_PAD), lambda i, t: (i, 0, t, 0)),
        pl.BlockSpec((None, MLA_HEADS, 1, ATT_VT_ROWS, ts), lambda i, t: (i, 0, t, 0, 0)),
    ]
    return pl.pallas_call(
        _even_in_kernel,
        grid=(b, nt),
        in_specs=[tile3(d),
                  pl.BlockSpec((None, 3, d), lambda i, t: (i, 0, 0)),
                  _const_spec((1, d)),
                  _const_spec(w_in_p.shape),
                  _const_spec((1, MLA_Q_RANK)),
                  _const_spec(w_uq_p.shape),
                  _const_spec((1, MLA_KV_RANK)),
                  _const_spec(w_ukv.shape),
                  tile3(LANES), tile3(LANES)],
        out_specs=out_specs,
        out_shape=out_shapes,
        compiler_params=_cparams(("arbitrary", "arbitrary")),
        name="even_in",
    )(x, mod, pre_g.reshape(1, d), w_in_p, q_norm_g.reshape(1, -1), w_uq_p,
      kv_norm_g.reshape(1, -1), w_ukv, cos_t, sin_t)


def _retention_kernel(dec_ref, q_ref, k_ref, v_ref, o_ref, sf_ref, kvb_ref, *, n_chunks, per_iter):
    c = RET_CHUNK
    dk = RET_QK_DIM
    lg = jax.nn.log_sigmoid(dec_ref[...])
    ri = lax.broadcasted_iota(jnp.int32, (c, LANES), 0)
    ci = lax.broadcasted_iota(jnp.int32, (c, LANES), 1)
    rif = ri.astype(F32)
    diff = rif - ci.astype(F32)
    row_h0 = ri < dk
    lane_h0 = ci < dk

    def dmat(lgf, lgb):
        return jnp.where(diff >= 0, jnp.exp(lgf * jnp.maximum(diff, 0.0)), jnp.exp(lgb * jnp.maximum(-diff, 0.0)))

    dec = jnp.concatenate([dmat(lg[0:1], lg[2:3]), dmat(lg[1:2], lg[3:4])], axis=0)
    qwf = jnp.concatenate([jnp.exp(lg[0:1] * (rif + 1.0)), jnp.exp(lg[1:2] * (rif + 1.0))], axis=0)
    qwb = jnp.concatenate([jnp.exp(lg[2:3] * (c - rif)), jnp.exp(lg[3:4] * (c - rif))], axis=0)
    lgf_lane = jnp.where(lane_h0, lg[0:1], lg[1:2])
    lgb_lane = jnp.where(lane_h0, lg[2:3], lg[3:4])
    kwf = jnp.exp(lgf_lane * (c - 1.0 - rif))
    kwb = jnp.exp(lgb_lane * rif)
    cdf = jnp.where(row_h0, jnp.exp(lg[0:1] * c), jnp.exp(lg[1:2] * c))
    cdb = jnp.where(row_h0, jnp.exp(lg[2:3] * c), jnp.exp(lg[3:4] * c))

    def rows_of(n):
        return pl.ds(pl.multiple_of(n * c, c), c)

    def pick_heads(m):
        return jnp.where(row_h0, m[:, :RET_V_DIM], m[:, RET_V_DIM:])

    def fwd_body(n, state_f):
        rows = rows_of(n)
        k = k_ref[rows, :].astype(F32)
        kw_t = jnp.concatenate([k * kwf, k * kwb], axis=1).T.astype(BF16)
        kv = jnp.dot(kw_t, v_ref[rows, :], preferred_element_type=F32)
        sf_ref[n] = state_f.astype(BF16)
        kvb_ref[n] = pick_heads(kv[c:, :])
        return state_f * cdf + pick_heads(kv[:c, :])

    def grouped(body):
        def group(i, state):
            for u in range(per_iter):
                state = body(i * per_iter + u, state)
            return state
        return group

    lax.fori_loop(0, n_chunks // per_iter, grouped(fwd_body), jnp.zeros((c, LANES), F32))

    def norm(o):
        mu = jnp.mean(o, axis=-1, keepdims=True)
        var = jnp.mean(jnp.square(o - mu), axis=-1, keepdims=True)
        return (o - mu) * lax.rsqrt(var + EPS)

    def bwd_body(t, state_b):
        n = n_chunks - 1 - t
        rows = rows_of(n)
        q = q_ref[rows, :].astype(F32)
        qm = jnp.concatenate([jnp.where(lane_h0, q, 0.0), jnp.where(lane_h0, 0.0, q)], axis=0)
        s = lax.dot_general(qm.astype(BF16), k_ref[rows, :], (((1,), (1,)), ((), ())),
                            preferred_element_type=F32)
        p = (s * dec).astype(BF16)
        v = v_ref[rows, :]
        inner0 = jnp.dot(p[:c], v[:, :RET_V_DIM], preferred_element_type=F32)
        inner1 = jnp.dot(p[c:], v[:, RET_V_DIM:], preferred_element_type=F32)
        lhs = jnp.concatenate([qm * qwf, qm * qwb], axis=1).astype(BF16)
        rhs = jnp.concatenate([sf_ref[n], state_b.astype(BF16)], axis=0)
        cross = jnp.dot(lhs, rhs, preferred_element_type=F32)
        o_ref[rows, :RET_V_DIM] = norm(inner0 + cross[:c]).astype(BF16)
        o_ref[rows, RET_V_DIM:] = norm(inner1 + cross[c:]).astype(BF16)
        return state_b * cdb + kvb_ref[n]

    lax.fori_loop(0, n_chunks // per_iter, grouped(bwd_body), jnp.zeros((c, LANES), F32))


def _retention(rq, rk, rv, dec_f, dec_b):
    b, s, _ = rq.shape
    n_chunks = s // RET_CHUNK
    pairs = RET_HEADS // 2
    dec = jnp.concatenate([dec_f.reshape(pairs, 2), dec_b.reshape(pairs, 2)], axis=1)
    dec = jnp.broadcast_to(dec[:, :, None], (pairs, 4, LANES)).astype(F32)
    seq = lambda w: pl.BlockSpec((None, s, w), lambda i, p: (i, 0, p))
    return pl.pallas_call(
        functools.partial(_retention_kernel, n_chunks=n_chunks, per_iter=math.gcd(n_chunks, RET_CHUNKS_PER_ITER)),
        grid=(b, pairs),
        in_specs=[pl.BlockSpec((None, 4, LANES), lambda i, p: (p, 0, 0)),
                  seq(2 * RET_QK_DIM), seq(2 * RET_QK_DIM), seq(2 * RET_V_DIM)],
        out_specs=seq(2 * RET_V_DIM),
        out_shape=jax.ShapeDtypeStruct((b, s, RET_WIDTH), BF16),
        scratch_shapes=[pltpu.VMEM((n_chunks, RET_CHUNK, LANES), BF16),
                        pltpu.VMEM((n_chunks, RET_CHUNK, LANES), F32)],
        compiler_params=_cparams(("arbitrary", "arbitrary")),
        name="retention",
    )(dec, rq, rk, rv)


def _attention_kernel(qt_ref, qt_next_ref, kc_ref, vt_ref, o_ref, s_ref, *, n_kv, tk):
    qt = qt_ref[...]
    tq = qt.shape[1]
    ck = ATT_PV_CHUNK
    vt_per_kv = tk // TOKEN_TILE
    chunks_per_vt = TOKEN_TILE // ck

    def scores(j, q):
        k = kc_ref[pl.ds(pl.multiple_of(j * tk, tk), tk), :]
        return jnp.dot(k, q, preferred_element_type=F32)

    def softmax_pv(slot, j, m, acc):
        m_new = jnp.maximum(m, jnp.max(s_ref[slot], axis=0, keepdims=True))
        acc = jnp.exp2(m - m_new) * acc
        for c in range(tk // ck):
            p = jnp.exp2(s_ref[slot, c * ck:(c + 1) * ck, :] - m_new).astype(BF16)
            lo = (c % chunks_per_vt) * ck
            vt = vt_ref[j * vt_per_kv + c // chunks_per_vt, :, lo:lo + ck]
            acc = acc + jnp.dot(vt, p, preferred_element_type=F32)
        return m_new, acc

    def pair(j, m, acc, j_ahead, q_ahead):
        s_ref[1] = scores(j + 1, qt)
        m, acc = softmax_pv(0, j, m, acc)
        s_ref[0] = scores(j_ahead, q_ahead)
        return softmax_pv(1, j + 1, m, acc)

    @pl.when(pl.program_id(2) == 0)
    def _():
        s_ref[0] = scores(0, qt)

    carry = (jnp.full((1, tq), -1e30, F32), jnp.zeros((ATT_VT_ROWS, tq), F32))
    carry = lax.fori_loop(0, n_kv // 2 - 1, lambda i, c: pair(2 * i, c[0], c[1], 2 * i + 2, qt), carry)
    _, acc = pair(n_kv - 2, carry[0], carry[1], 0, qt_next_ref[...])
    o_ref[...] = (acc[:MLA_V_DIM] / acc[MLA_V_DIM:MLA_V_DIM + 1]).T.astype(BF16)


def _attention(qt, kc, vt):
    b, h, _, s = qt.shape
    n_vt = vt.shape[2]
    tq = min(ATT_Q_TILE, s)
    tk = ATT_KV_TILE if s % (2 * ATT_KV_TILE) == 0 else TOKEN_TILE
    assert s % (2 * tk) == 0 and s % tq == 0
    nq = s // tq
    return pl.pallas_call(
        functools.partial(_attention_kernel, n_kv=s // tk, tk=tk),
        grid=(b, h, nq),
        in_specs=[pl.BlockSpec((None, None, ATT_QK_PAD, tq), lambda i, j, t: (i, j, 0, t)),
                  pl.BlockSpec((None, None, ATT_QK_PAD, tq), lambda i, j, t: (i, j, 0, jnp.minimum(t + 1, nq - 1))),
                  pl.BlockSpec((None, None, s, ATT_QK_PAD), lambda i, j, t: (i, j, 0, 0)),
                  pl.BlockSpec((None, None, n_vt, ATT_VT_ROWS, TOKEN_TILE), lambda i, j, t: (i, j, 0, 0, 0))],
        out_specs=pl.BlockSpec((None, tq, MLA_V_DIM), lambda i, j, t: (i, t, j)),
        out_shape=jax.ShapeDtypeStruct((b, s, MLA_WIDTH), BF16),
        scratch_shapes=[pltpu.VMEM((2, tk, tq), F32)],
        compiler_params=_cparams(("arbitrary", "arbitrary", "arbitrary")),
        name="mla_attention",
    )(qt, qt, kc, vt)


def _even_out_kernel(x_ref, ret_ref, att_ref, gate_ref, w_ref, pg_ref, mod_ref, o_ref):
    m_ret = ret_ref[...] * gate_ref[:, :RET_WIDTH]
    m_att = att_ref[...] * gate_ref[:, RET_WIDTH:]
    y = (jnp.dot(m_ret, w_ref[:RET_WIDTH, :], preferred_element_type=F32)
         + jnp.dot(m_att, w_ref[RET_WIDTH:, :], preferred_element_type=F32))
    o_ref[...] = x_ref[...] + mod_ref[2:3, :] * _rms(y, pg_ref[...])


def _even_out(x, ret, att, gates, w_out, post_g, mod):
    b, s, d = x.shape
    ts = TOKEN_TILE
    tile3 = lambda w: pl.BlockSpec((None, ts, w), lambda i, t: (i, t, 0))
    return pl.pallas_call(
        _even_out_kernel,
        grid=(b, s // ts),
        in_specs=[tile3(d), tile3(RET_WIDTH), tile3(MLA_WIDTH), tile3(RET_WIDTH + MLA_WIDTH),
                  _const_spec(w_out.shape), _const_spec((1, d)),
                  pl.BlockSpec((None, 3, d), lambda i, t: (i, 0, 0))],
        out_specs=tile3(d),
        out_shape=jax.ShapeDtypeStruct((b, s, d), F32),
        compiler_params=_cparams(("arbitrary", "arbitrary")),
        name="even_out",
    )(x, ret, att, gates, w_out, post_g.reshape(1, d), mod)


def _odd_kernel(xp_ref, x_ref, xn_ref, mod_ref, pg_ref, win_ref, bin_ref, dww_ref, dwb_ref,
                lng_ref, lnb_ref, wout_ref, postg_ref, o_ref, u_ref, c_ref, *, n_tiles):
    t = pl.program_id(1)
    ts, d = x_ref.shape
    halo = CONV_HALO
    x = x_ref[...]
    xa = jnp.concatenate([xp_ref[...], x, xn_ref[...]], axis=0)
    h = (_rms(xa, pg_ref[...]) * (1.0 + mod_ref[1:2, :]) + mod_ref[0:1, :]).astype(BF16)
    row = lax.broadcasted_iota(jnp.int32, (ts + 2 * halo, 1), 0)
    inside = jnp.logical_and(jnp.logical_or(row >= halo, t > 0),
                             jnp.logical_or(row < ts + halo, t < n_tiles - 1))
    off = halo - CONV_KERNEL // 2
    gate = _silu(jnp.dot(h[halo:halo + ts], win_ref[:, 2 * d:], preferred_element_type=F32) + bin_ref[:, 2 * d:])

    for cb in range(0, d, ODD_COL_BLOCK):
        cols = slice(cb, cb + ODD_COL_BLOCK)
        gcols = slice(d + cb, d + cb + ODD_COL_BLOCK)
        za = jnp.dot(h, win_ref[:, cols], preferred_element_type=F32) + bin_ref[:, cols]
        zb = jnp.dot(h, win_ref[:, gcols], preferred_element_type=F32) + bin_ref[:, gcols]
        u = jnp.where(inside, za * jax.nn.sigmoid(zb), 0.0)
        for j in range(ODD_COL_BLOCK // LANES):
            sl = cb // LANES + j
            cs = slice(sl * LANES, (sl + 1) * LANES)
            u_ref[sl] = u[:, j * LANES:(j + 1) * LANES]
            for r0 in range(0, ts, CONV_ROW_CHUNK):
                acc = jnp.broadcast_to(dwb_ref[:, cs], (CONV_ROW_CHUNK, LANES))
                for tap in range(CONV_KERNEL):
                    acc = acc + u_ref[sl, pl.ds(r0 + off + tap, CONV_ROW_CHUNK), :] * dww_ref[tap:tap + 1, cs]
                c_ref[pl.ds(r0, CONV_ROW_CHUNK), cs] = acc

    cv = c_ref[...]
    mu = jnp.mean(cv, axis=-1, keepdims=True)
    var = jnp.mean(jnp.square(cv - mu), axis=-1, keepdims=True)
    ln = (cv - mu) * lax.rsqrt(var + EPS) * lng_ref[...] + lnb_ref[...]
    m = (_silu(ln) * gate).astype(BF16)
    y = jnp.dot(m, wout_ref[...], preferred_element_type=F32)
    o_ref[...] = x + mod_ref[2:3, :] * _rms(y, postg_ref[...])


def _odd_layer(x, mod, pre_g, post_g, w_in, b_in, dw_w, dw_b, ln_g, ln_b, w_out):
    b, s, d = x.shape
    ts = TOKEN_TILE
    nt = s // ts
    hb = ts // CONV_HALO
    n_hb = s // CONV_HALO
    row = lambda v: v.reshape(1, -1)
    return pl.pallas_call(
        functools.partial(_odd_kernel, n_tiles=nt),
        grid=(b, nt),
        in_specs=[pl.BlockSpec((None, CONV_HALO, d), lambda i, t: (i, jnp.maximum(t * hb - 1, 0), 0)),
                  pl.BlockSpec((None, ts, d), lambda i, t: (i, t, 0)),
                  pl.BlockSpec((None, CONV_HALO, d), lambda i, t: (i, jnp.minimum((t + 1) * hb, n_hb - 1), 0)),
                  pl.BlockSpec((None, 3, d), lambda i, t: (i, 0, 0)),
                  _const_spec((1, d)), _const_spec(w_in.shape), _const_spec((1, 3 * d)),
                  _const_spec(dw_w.shape), _const_spec((1, d)), _const_spec((1, d)), _const_spec((1, d)),
                  _const_spec(w_out.shape), _const_spec((1, d))],
        out_specs=pl.BlockSpec((None, ts, d), lambda i, t: (i, t, 0)),
        out_shape=jax.ShapeDtypeStruct((b, s, d), F32),
        scratch_shapes=[pltpu.VMEM((d // LANES, ts + 2 * CONV_HALO, LANES), F32), pltpu.VMEM((ts, d), F32)],
        compiler_params=_cparams(("arbitrary", "arbitrary")),
        name="odd_layer",
    )(x, x, x, mod, row(pre_g), w_in, row(b_in), dw_w, row(dw_b), row(ln_g), row(ln_b), w_out, row(post_g))


def _pack_even_weights(w_in, w_uq, w_ukv, w_out):
    d = w_in.shape[0]
    hq = RET_HEADS * RET_QK_DIM
    sizes = (hq, hq, RET_WIDTH, RET_WIDTH, MLA_Q_RANK, MLA_KV_RANK, MLA_ROPE_DIM, MLA_WIDTH)
    pieces, start = [], 0
    for size in sizes:
        pieces.append(w_in[:, start:start + size])
        start += size
    pieces.insert(7, jnp.zeros((d, LANES - MLA_ROPE_DIM), w_in.dtype))
    w_in_p = jnp.concatenate(pieces, axis=1).astype(BF16)
    wq = w_uq.reshape(MLA_Q_RANK, MLA_HEADS, MLA_NOPE_DIM + MLA_ROPE_DIM)
    w_uq_p = jnp.concatenate([wq[:, :, :MLA_NOPE_DIM].reshape(MLA_Q_RANK, -1),
                              wq[:, :, MLA_NOPE_DIM:].reshape(MLA_Q_RANK, -1)], axis=1).astype(BF16)
    return w_in_p, w_uq_p, w_ukv.astype(BF16), w_out.astype(BF16)


def kernel(x, c, positions, ada_w, ada_b, pre_g, post_g, ev_w_in, ev_dec_f, ev_dec_b, ev_q_norm_g, ev_w_uq, ev_kv_norm_g, ev_w_ukv, ev_w_out, od_w_in, od_b_in, od_dw_w, od_dw_b, od_ln_g, od_ln_b, od_w_out):
    depth = ada_w.shape[0]
    s = x.shape[1]
    assert s % TOKEN_TILE == 0 and TOKEN_TILE % RET_CHUNK == 0
    mod = _modulation(c, ada_w, ada_b)
    cos_t, sin_t = _rope_tables(positions)
    for layer in range(depth):
        i = layer // 2
        if layer % 2 == 0:
            w_in_p, w_uq_p, w_ukv, w_out = _pack_even_weights(ev_w_in[i], ev_w_uq[i], ev_w_ukv[i], ev_w_out[i])
            rq, rk, rv, gates, qt, kc, vt = _even_in(x, mod[layer], pre_g[layer], w_in_p, ev_q_norm_g[i], w_uq_p,
                                                     ev_kv_norm_g[i], w_ukv, cos_t, sin_t)
            ret = _retention(rq, rk, rv, ev_dec_f[i], ev_dec_b[i])
            att = _attention(qt, kc, vt)
            x = _even_out(x, ret, att, gates, w_out, post_g[layer], mod[layer])
        else:
            x = _odd_layer(x, mod[layer], pre_g[layer], post_g[layer], od_w_in[i].astype(BF16), od_b_in[i],
                           od_dw_w[i], od_dw_b[i], od_ln_g[i], od_ln_b[i], od_w_out[i].astype(BF16))
    return x
```

```python
import functools
import math

import jax
import jax.numpy as jnp
from jax import lax
from jax.experimental import pallas as pl
from jax.experimental.pallas import tpu as pltpu

F32 = jnp.float32
BF16 = jnp.bfloat16

RET_HEADS = 4
RET_QK_DIM = 64
RET_V_DIM = 128
RET_CHUNK = 128
MLA_HEADS = 4
MLA_Q_RANK = 384
MLA_KV_RANK = 256
MLA_NOPE_DIM = 128
MLA_ROPE_DIM = 64
MLA_V_DIM = 128
ROPE_DIM = 64
ROPE_BASE = 10000.0
CONV_KERNEL = 31
EPS = 1e-6

RET_WIDTH = RET_HEADS * RET_V_DIM
MLA_WIDTH = MLA_HEADS * MLA_V_DIM
RET_SCALE = RET_QK_DIM ** -0.5
ATT_SCALE_LOG2 = (MLA_NOPE_DIM + MLA_ROPE_DIM) ** -0.5 * math.log2(math.e)

LANES = 128
MXU_DIM = 256
ATT_QK_PAD = MXU_DIM
ROPE_PACK = LANES // (ROPE_DIM // 2)
CONV_HALO = 16

TOKEN_TILE = 512
WIDE_TOKEN_TILE = 1024
ATT_Q_TILE = 1024
ATT_KV_TILE = 2 * TOKEN_TILE
ATT_PV_CHUNK = MXU_DIM
ATT_VT_ROWS = MLA_V_DIM + 16
CONV_ROW_CHUNK = 64
ODD_COL_BLOCK = MXU_DIM
RET_CHUNKS_PER_ITER = 16

V7X_VMEM_BYTES = 64 * 1024 * 1024
VMEM_LIMIT_BYTES = V7X_VMEM_BYTES - 8 * 1024 * 1024

_C_RQ = 0
_C_RK = _C_RQ + RET_HEADS * RET_QK_DIM
_C_RV = _C_RK + RET_HEADS * RET_QK_DIM
_C_RG = _C_RV + RET_WIDTH
_C_CQ = _C_RG + RET_WIDTH
_C_CKV = _C_CQ + MLA_Q_RANK
_C_KR = _C_CKV + MLA_KV_RANK
_C_MG = _C_KR + LANES
_C_END = _C_MG + MLA_WIDTH


def _cparams(semantics):
    return pltpu.CompilerParams(dimension_semantics=semantics, vmem_limit_bytes=VMEM_LIMIT_BYTES)


def _const_spec(shape):
    nd = len(shape)
    return pl.BlockSpec(shape, lambda *_: (0,) * nd, pipeline_mode=pl.Buffered(1))


def _silu(v):
    return v * jax.nn.sigmoid(v)


def _rms(v, g):
    return v * lax.rsqrt(jnp.mean(v * v, axis=-1, keepdims=True) + EPS) * g


def _mod_kernel(c_ref, w_ref, b_ref, o_ref):
    c = c_ref[...]
    o_ref[...] = jnp.dot(_silu(c).astype(BF16), w_ref[...].astype(BF16),
                         preferred_element_type=F32) + b_ref[...]


def _modulation(c, ada_w, ada_b):
    depth, d, d3 = ada_w.shape
    b = c.shape[0]
    rows = -(-b // 8) * 8
    c_pad = jnp.pad(c, ((0, rows - b), (0, 0)))
    out = pl.pallas_call(
        _mod_kernel,
        grid=(depth, d3 // d),
        in_specs=[pl.BlockSpec((rows, d), lambda l, j: (0, 0)),
                  pl.BlockSpec((None, d, d), lambda l, j: (l, 0, j)),
                  pl.BlockSpec((None, 1, d), lambda l, j: (l, 0, j))],
        out_specs=pl.BlockSpec((None, rows, d), lambda l, j: (l, 0, j)),
        out_shape=jax.ShapeDtypeStruct((depth, rows, d3), F32),
        compiler_params=_cparams(("arbitrary", "arbitrary")),
        name="adaln_mod",
    )(c_pad, ada_w, ada_b.reshape(depth, 1, d3))
    return out[:, :b].reshape(depth, b, d3 // d, d)


def _rope_table_kernel(pos_ref, invf_ref, cos_ref, sin_ref):
    rows = pos_ref.shape[0]
    half = ROPE_DIM // 2
    ang = pos_ref[...].astype(F32) * invf_ref[...]
    lane = lax.broadcasted_iota(jnp.int32, ang.shape, 1)
    for table, dst in ((jnp.cos(ang), cos_ref), (jnp.sin(ang), sin_ref)):
        for q in range(ROPE_PACK):
            v = pltpu.roll(table, LANES - half * q, 1) if q else table
            v = jnp.where(lane < half, v, 0.0)
            v = v + pltpu.roll(v, half, 1)
            v = v + pltpu.roll(v, 2 * half, 1)
            dst[pl.ds(q, rows, stride=ROPE_PACK), :] = v


def _rope_tables(positions):
    b, s = positions.shape
    half = ROPE_DIM // 2
    inv_freq = ROPE_BASE ** (-jnp.arange(0, ROPE_DIM, 2, dtype=F32) / ROPE_DIM)
    invf = jnp.tile(inv_freq, ROPE_PACK).reshape(1, LANES)
    rows = s // ROPE_PACK
    pos = jnp.repeat(positions.reshape(b, rows, ROPE_PACK), half, axis=-1)
    tr = min(TOKEN_TILE, rows)
    return pl.pallas_call(
        _rope_table_kernel,
        grid=(b, rows // tr),
        in_specs=[pl.BlockSpec((None, tr, LANES), lambda i, t: (i, t, 0)),
                  pl.BlockSpec((1, LANES), lambda i, t: (0, 0))],
        out_specs=[pl.BlockSpec((None, ROPE_PACK * tr, LANES), lambda i, t: (i, t, 0))] * 2,
        out_shape=[jax.ShapeDtypeStruct((b, s, LANES), F32)] * 2,
        compiler_params=_cparams(("arbitrary", "arbitrary")),
        name="rope_tables",
    )(pos, invf)


def _rope_slab(v, cos, sin, first_half):
    half = ROPE_DIM // 2
    partner = jnp.where(first_half, -pltpu.roll(v, LANES - half, 1), pltpu.roll(v, half, 1))
    return v * cos + partner * sin


def _even_in_kernel(x_ref, mod_ref, pg_ref, win_ref, qg_ref, wuq_ref, kvg_ref, wukv_ref,
                    cos_ref, sin_ref,
                    rq_ref, rk_ref, rv_ref, gate_ref, qt_ref, kc_ref, vt_ref):
    x = x_ref[...]
    h = (_rms(x, pg_ref[...]) * (1.0 + mod_ref[1:2, :]) + mod_ref[0:1, :]).astype(BF16)

    def proj(lo, hi):
        return jnp.dot(h, win_ref[:, lo:hi], preferred_element_type=F32)

    cos = cos_ref[...]
    sin = sin_ref[...]
    lane = lax.broadcasted_iota(jnp.int32, cos.shape, 1)
    first_half = (lane % ROPE_DIM) < (ROPE_DIM // 2)
    rope = functools.partial(_rope_slab, cos=cos, sin=sin, first_half=first_half)

    z_lat = proj(_C_CQ, _C_MG)
    cqn = _rms(z_lat[:, :MLA_Q_RANK], qg_ref[...])
    q = jnp.dot(cqn.astype(BF16), wuq_ref[...], preferred_element_type=F32)
    kvn = _rms(z_lat[:, MLA_Q_RANK:MLA_Q_RANK + MLA_KV_RANK], kvg_ref[...])
    kv = jnp.dot(kvn.astype(BF16), wukv_ref[...], preferred_element_type=F32)
    k_rope = rope(z_lat[:, _C_KR - _C_CQ:]).astype(BF16)

    z_ret = proj(_C_RQ, _C_CQ)
    for j in range(RET_HEADS * RET_QK_DIM // LANES):
        sl = slice(j * LANES, (j + 1) * LANES)
        rq_ref[:, sl] = rope(z_ret[:, _C_RQ + j * LANES:_C_RQ + (j + 1) * LANES]).astype(BF16)
        rk_ref[:, sl] = (rope(z_ret[:, _C_RK + j * LANES:_C_RK + (j + 1) * LANES]) * RET_SCALE).astype(BF16)
    rv_ref[...] = z_ret[:, _C_RV:_C_RG].astype(BF16)
    gate_ref[:, :RET_WIDTH] = _silu(z_ret[:, _C_RG:_C_CQ]).astype(BF16)
    gate_ref[:, RET_WIDTH:] = _silu(proj(_C_MG, _C_END)).astype(BF16)

    nope_w = MLA_HEADS * MLA_NOPE_DIM
    q_rope = [rope(q[:, nope_w + j * LANES:nope_w + (j + 1) * LANES]) for j in range(2)]
    low = lane < ROPE_DIM
    ones_rows = jnp.ones((ATT_VT_ROWS - MLA_V_DIM, x.shape[0]), BF16)
    for hd in range(MLA_HEADS):
        qn = q[:, hd * MLA_NOPE_DIM:(hd + 1) * MLA_NOPE_DIM] * ATT_SCALE_LOG2
        slab = q_rope[hd // 2]
        if hd % 2:
            slab = pltpu.roll(slab, ROPE_DIM, 1)
        qr = jnp.where(low, slab, 0.0) * ATT_SCALE_LOG2
        qt_ref[hd, :MLA_NOPE_DIM, :] = qn.T.astype(BF16)
        qt_ref[hd, MLA_NOPE_DIM:, :] = qr.T.astype(BF16)
        base = hd * (MLA_NOPE_DIM + MLA_V_DIM)
        kc_ref[hd, :, :MLA_NOPE_DIM] = kv[:, base:base + MLA_NOPE_DIM].astype(BF16)
        kc_ref[hd, :, MLA_NOPE_DIM:] = k_rope
        vt_ref[hd, 0, :MLA_V_DIM, :] = kv[:, base + MLA_NOPE_DIM:base + MLA_NOPE_DIM + MLA_V_DIM].T.astype(BF16)
        vt_ref[hd, 0, MLA_V_DIM:, :] = ones_rows


def _even_in(x, mod, pre_g, w_in_p, q_norm_g, w_uq_p, kv_norm_g, w_ukv, cos_t, sin_t):
    b, s, d = x.shape
    ts = TOKEN_TILE
    nt = s // ts
    hq = RET_HEADS * RET_QK_DIM
    tile3 = lambda w: pl.BlockSpec((None, ts, w), lambda i, t: (i, t, 0))
    out_shapes = [
        jax.ShapeDtypeStruct((b, s, hq), BF16),
        jax.ShapeDtypeStruct((b, s, hq), BF16),
        jax.ShapeDtypeStruct((b, s, RET_WIDTH), BF16),
        jax.ShapeDtypeStruct((b, s, RET_WIDTH + MLA_WIDTH), BF16),
        jax.ShapeDtypeStruct((b, MLA_HEADS, ATT_QK_PAD, s), BF16),
        jax.ShapeDtypeStruct((b, MLA_HEADS, s, ATT_QK_PAD), BF16),
        jax.ShapeDtypeStruct((b, MLA_HEADS, nt, ATT_VT_ROWS, ts), BF16),
    ]
    out_specs = [
        tile3(hq), tile3(hq), tile3(RET_WIDTH), tile3(RET_WIDTH + MLA_WIDTH),
        pl.BlockSpec((None, MLA_HEADS, ATT_QK_PAD, ts), lambda i, t: (i, 0, 0, t)),
        pl.BlockSpec((None, MLA_HEADS, ts, ATT_QK_PAD), lambda i, t: (i, 0, t, 0)),
        pl.BlockSpec((None, MLA_HEADS, 1, ATT_VT_ROWS, ts), lambda i, t: (i, 0, t, 0, 0)),
    ]
    return pl.pallas_call(
        _even_in_kernel,
        grid=(b, nt),
        in_specs=[tile3(d),
                  pl.BlockSpec((None, 3, d), lambda i, t: (i, 0, 0)),
                  _const_spec((1, d)),
                  _const_spec(w_in_p.shape),
                  _const_spec((1, MLA_Q_RANK)),
                  _const_spec(w_uq_p.shape),
                  _const_spec((1, MLA_KV_RANK)),
                  _const_spec(w_ukv.shape),
                  tile3(LANES), tile3(LANES)],
        out_specs=out_specs,
        out_shape=out_shapes,
        compiler_params=_cparams(("arbitrary", "arbitrary")),
        name="even_in",
    )(x, mod, pre_g.reshape(1, d), w_in_p, q_norm_g.reshape(1, -1), w_uq_p,
      kv_norm_g.reshape(1, -1), w_ukv, cos_t, sin_t)


def _retention_kernel(dec_ref, q_ref, k_ref, v_ref, o_ref, sf_ref, kvb_ref, *, n_chunks, per_iter):
    c = RET_CHUNK
    dk = RET_QK_DIM
    lg = jax.nn.log_sigmoid(dec_ref[...])
    ri = lax.broadcasted_iota(jnp.int32, (c, LANES), 0)
    ci = lax.broadcasted_iota(jnp.int32, (c, LANES), 1)
    rif = ri.astype(F32)
    diff = rif - ci.astype(F32)
    row_h0 = ri < dk
    lane_h0 = ci < dk

    def dmat(lgf, lgb):
        return jnp.where(diff >= 0, jnp.exp(lgf * jnp.maximum(diff, 0.0)), jnp.exp(lgb * jnp.maximum(-diff, 0.0)))

    dec = jnp.concatenate([dmat(lg[0:1], lg[2:3]), dmat(lg[1:2], lg[3:4])], axis=0)
    qwf = jnp.concatenate([jnp.exp(lg[0:1] * (rif + 1.0)), jnp.exp(lg[1:2] * (rif + 1.0))], axis=0)
    qwb = jnp.concatenate([jnp.exp(lg[2:3] * (c - rif)), jnp.exp(lg[3:4] * (c - rif))], axis=0)
    lgf_lane = jnp.where(lane_h0, lg[0:1], lg[1:2])
    lgb_lane = jnp.where(lane_h0, lg[2:3], lg[3:4])
    kwf = jnp.exp(lgf_lane * (c - 1.0 - rif))
    kwb = jnp.exp(lgb_lane * rif)
    cdf = jnp.where(row_h0, jnp.exp(lg[0:1] * c), jnp.exp(lg[1:2] * c))
    cdb = jnp.where(row_h0, jnp.exp(lg[2:3] * c), jnp.exp(lg[3:4] * c))

    def rows_of(n):
        return pl.ds(pl.multiple_of(n * c, c), c)

    def pick_heads(m):
        return jnp.where(row_h0, m[:, :RET_V_DIM], m[:, RET_V_DIM:])

    def fwd_body(n, state_f):
        rows = rows_of(n)
        k = k_ref[rows, :].astype(F32)
        kw_t = jnp.concatenate([k * kwf, k * kwb], axis=1).T.astype(BF16)
        kv = jnp.dot(kw_t, v_ref[rows, :], preferred_element_type=F32)
        sf_ref[n] = state_f.astype(BF16)
        kvb_ref[n] = pick_heads(kv[c:, :])
        return state_f * cdf + pick_heads(kv[:c, :])

    def grouped(body):
        def group(i, state):
            for u in range(per_iter):
                state = body(i * per_iter + u, state)
            return state
        return group

    lax.fori_loop(0, n_chunks // per_iter, grouped(fwd_body), jnp.zeros((c, LANES), F32))

    def norm(o):
        mu = jnp.mean(o, axis=-1, keepdims=True)
        var = jnp.mean(jnp.square(o - mu), axis=-1, keepdims=True)
        return (o - mu) * lax.rsqrt(var + EPS)

    def bwd_body(t, state_b):
        n = n_chunks - 1 - t
        rows = rows_of(n)
        q = q_ref[rows, :].astype(F32)
        qm = jnp.concatenate([jnp.where(lane_h0, q, 0.0), jnp.where(lane_h0, 0.0, q)], axis=0)
        s = lax.dot_general(qm.astype(BF16), k_ref[rows, :], (((1,), (1,)), ((), ())),
                            preferred_element_type=F32)
        p = (s * dec).astype(BF16)
        v = v_ref[rows, :]
        inner0 = jnp.dot(p[:c], v[:, :RET_V_DIM], preferred_element_type=F32)
        inner1 = jnp.dot(p[c:], v[:, RET_V_DIM:], preferred_element_type=F32)
        lhs = jnp.concatenate([qm * qwf, qm * qwb], axis=1).astype(BF16)
        rhs = jnp.concatenate([sf_ref[n], state_b.astype(BF16)], axis=0)
        cross = jnp.dot(lhs, rhs, preferred_element_type=F32)
        o_ref[rows, :RET_V_DIM] = norm(inner0 + cross[:c]).astype(BF16)
        o_ref[rows, RET_V_DIM:] = norm(inner1 + cross[c:]).astype(BF16)
        return state_b * cdb + kvb_ref[n]

    lax.fori_loop(0, n_chunks // per_iter, grouped(bwd_body), jnp.zeros((c, LANES), F32))


def _retention(rq, rk, rv, dec_f, dec_b):
    b, s, _ = rq.shape
    n_chunks = s // RET_CHUNK
    pairs = RET_HEADS // 2
    dec = jnp.concatenate([dec_f.reshape(pairs, 2), dec_b.reshape(pairs, 2)], axis=1)
    dec = jnp.broadcast_to(dec[:, :, None], (pairs, 4, LANES)).astype(F32)
    seq = lambda w: pl.BlockSpec((None, s, w), lambda i, p: (i, 0, p))
    return pl.pallas_call(
        functools.partial(_retention_kernel, n_chunks=n_chunks, per_iter=math.gcd(n_chunks, RET_CHUNKS_PER_ITER)),
        grid=(b, pairs),
        in_specs=[pl.BlockSpec((None, 4, LANES), lambda i, p: (p, 0, 0)),
                  seq(2 * RET_QK_DIM), seq(2 * RET_QK_DIM), seq(2 * RET_V_DIM)],
        out_specs=seq(2 * RET_V_DIM),
        out_shape=jax.ShapeDtypeStruct((b, s, RET_WIDTH), BF16),
        scratch_shapes=[pltpu.VMEM((n_chunks, RET_CHUNK, LANES), BF16),
                        pltpu.VMEM((n_chunks, RET_CHUNK, LANES), F32)],
        compiler_params=_cparams(("arbitrary", "arbitrary")),
        name="retention",
    )(dec, rq, rk, rv)


def _attention_kernel(qt_ref, qt_next_ref, kc_ref, vt_ref, o_ref, s_ref, *, n_kv, tk):
    qt = qt_ref[...]
    tq = qt.shape[1]
    ck = ATT_PV_CHUNK
    vt_per_kv = tk // TOKEN_TILE
    chunks_per_vt = TOKEN_TILE // ck

    def scores(j, q):
        k = kc_ref[pl.ds(pl.multiple_of(j * tk, tk), tk), :]
        return jnp.dot(k, q, preferred_element_type=F32)

    def softmax_pv(slot, j, m, acc):
        m_new = jnp.maximum(m, jnp.max(s_ref[slot], axis=0, keepdims=True))
        acc = jnp.exp2(m - m_new) * acc
        for c in range(tk // ck):
            p = jnp.exp2(s_ref[slot, c * ck:(c + 1) * ck, :] - m_new).astype(BF16)
            lo = (c % chunks_per_vt) * ck
            vt = vt_ref[j * vt_per_kv + c // chunks_per_vt, :, lo:lo + ck]
            acc = acc + jnp.dot(vt, p, preferred_element_type=F32)
        return m_new, acc

    def pair(j, m, acc, j_ahead, q_ahead):
        s_ref[1] = scores(j + 1, qt)
        m, acc = softmax_pv(0, j, m, acc)
        s_ref[0] = scores(j_ahead, q_ahead)
        return softmax_pv(1, j + 1, m, acc)

    @pl.when(pl.program_id(2) == 0)
    def _():
        s_ref[0] = scores(0, qt)

    carry = (jnp.full((1, tq), -1e30, F32), jnp.zeros((ATT_VT_ROWS, tq), F32))
    carry = lax.fori_loop(0, n_kv // 2 - 1, lambda i, c: pair(2 * i, c[0], c[1], 2 * i + 2, qt), carry)
    _, acc = pair(n_kv - 2, carry[0], carry[1], 0, qt_next_ref[...])
    o_ref[...] = (acc[:MLA_V_DIM] / acc[MLA_V_DIM:MLA_V_DIM + 1]).T.astype(BF16)


def _attention(qt, kc, vt):
    b, h, _, s = qt.shape
    n_vt = vt.shape[2]
    tq = min(ATT_Q_TILE, s)
    tk = ATT_KV_TILE if s % (2 * ATT_KV_TILE) == 0 else TOKEN_TILE
    assert s % (2 * tk) == 0 and s % tq == 0
    nq = s // tq
    return pl.pallas_call(
        functools.partial(_attention_kernel, n_kv=s // tk, tk=tk),
        grid=(b, h, nq),
        in_specs=[pl.BlockSpec((None, None, ATT_QK_PAD, tq), lambda i, j, t: (i, j, 0, t)),
                  pl.BlockSpec((None, None, ATT_QK_PAD, tq), lambda i, j, t: (i, j, 0, jnp.minimum(t + 1, nq - 1))),
                  pl.BlockSpec((None, None, s, ATT_QK_PAD), lambda i, j, t: (i, j, 0, 0)),
                  pl.BlockSpec((None, None, n_vt, ATT_VT_ROWS, TOKEN_TILE), lambda i, j, t: (i, j, 0, 0, 0))],
        out_specs=pl.BlockSpec((None, tq, MLA_V_DIM), lambda i, j, t: (i, t, j)),
        out_shape=jax.ShapeDtypeStruct((b, s, MLA_WIDTH), BF16),
        scratch_shapes=[pltpu.VMEM((2, tk, tq), F32)],
        compiler_params=_cparams(("arbitrary", "arbitrary", "arbitrary")),
        name="mla_attention",
    )(qt, qt, kc, vt)


def _even_out_kernel(x_ref, ret_ref, att_ref, gate_ref, w_ref, pg_ref, mod_ref, o_ref):
    m_ret = ret_ref[...] * gate_ref[:, :RET_WIDTH]
    m_att = att_ref[...] * gate_ref[:, RET_WIDTH:]
    y = (jnp.dot(m_ret, w_ref[:RET_WIDTH, :], preferred_element_type=F32)
         + jnp.dot(m_att, w_ref[RET_WIDTH:, :], preferred_element_type=F32))
    o_ref[...] = x_ref[...] + mod_ref[2:3, :] * _rms(y, pg_ref[...])


def _even_out(x, ret, att, gates, w_out, post_g, mod):
    b, s, d = x.shape
    ts = math.gcd(s, WIDE_TOKEN_TILE)
    tile3 = lambda w: pl.BlockSpec((None, ts, w), lambda i, t: (i, t, 0))
    return pl.pallas_call(
        _even_out_kernel,
        grid=(b, s // ts),
        in_specs=[tile3(d), tile3(RET_WIDTH), tile3(MLA_WIDTH), tile3(RET_WIDTH + MLA_WIDTH),
                  _const_spec(w_out.shape), _const_spec((1, d)),
                  pl.BlockSpec((None, 3, d), lambda i, t: (i, 0, 0))],
        out_specs=tile3(d),
        out_shape=jax.ShapeDtypeStruct((b, s, d), F32),
        compiler_params=_cparams(("arbitrary", "arbitrary")),
        name="even_out",
    )(x, ret, att, gates, w_out, post_g.reshape(1, d), mod)


def _odd_kernel(xp_ref, x_ref, xn_ref, mod_ref, pg_ref, win_ref, bin_ref, dww_ref, dwb_ref,
                lng_ref, lnb_ref, wout_ref, postg_ref, o_ref, u_ref, c_ref, *, n_tiles):
    t = pl.program_id(1)
    ts, d = x_ref.shape
    halo = CONV_HALO
    x = x_ref[...]
    xa = jnp.concatenate([xp_ref[...], x, xn_ref[...]], axis=0)
    h = (_rms(xa, pg_ref[...]) * (1.0 + mod_ref[1:2, :]) + mod_ref[0:1, :]).astype(BF16)
    row = lax.broadcasted_iota(jnp.int32, (ts + 2 * halo, 1), 0)
    inside = jnp.logical_and(jnp.logical_or(row >= halo, t > 0),
                             jnp.logical_or(row < ts + halo, t < n_tiles - 1))
    off = halo - CONV_KERNEL // 2
    gate = _silu(jnp.dot(h[halo:halo + ts], win_ref[:, 2 * d:], preferred_element_type=F32) + bin_ref[:, 2 * d:])

    for cb in range(0, d, ODD_COL_BLOCK):
        cols = slice(cb, cb + ODD_COL_BLOCK)
        gcols = slice(d + cb, d + cb + ODD_COL_BLOCK)
        za = jnp.dot(h, win_ref[:, cols], preferred_element_type=F32) + bin_ref[:, cols]
        zb = jnp.dot(h, win_ref[:, gcols], preferred_element_type=F32) + bin_ref[:, gcols]
        u = jnp.where(inside, za * jax.nn.sigmoid(zb), 0.0)
        for j in range(ODD_COL_BLOCK // LANES):
            sl = cb // LANES + j
            cs = slice(sl * LANES, (sl + 1) * LANES)
            u_ref[sl] = u[:, j * LANES:(j + 1) * LANES]
            for r0 in range(0, ts, CONV_ROW_CHUNK):
                acc = jnp.broadcast_to(dwb_ref[:, cs], (CONV_ROW_CHUNK, LANES))
                for tap in range(CONV_KERNEL):
                    acc = acc + u_ref[sl, pl.ds(r0 + off + tap, CONV_ROW_CHUNK), :] * dww_ref[tap:tap + 1, cs]
                c_ref[pl.ds(r0, CONV_ROW_CHUNK), cs] = acc

    cv = c_ref[...]
    mu = jnp.mean(cv, axis=-1, keepdims=True)
    var = jnp.mean(jnp.square(cv - mu), axis=-1, keepdims=True)
    ln = (cv - mu) * lax.rsqrt(var + EPS) * lng_ref[...] + lnb_ref[...]
    m = (_silu(ln) * gate).astype(BF16)
    y = jnp.dot(m, wout_ref[...], preferred_element_type=F32)
    o_ref[...] = x + mod_ref[2:3, :] * _rms(y, postg_ref[...])


def _odd_layer(x, mod, pre_g, post_g, w_in, b_in, dw_w, dw_b, ln_g, ln_b, w_out):
    b, s, d = x.shape
    ts = math.gcd(s, WIDE_TOKEN_TILE)
    nt = s // ts
    hb = ts // CONV_HALO
    n_hb = s // CONV_HALO
    row = lambda v: v.reshape(1, -1)
    return pl.pallas_call(
        functools.partial(_odd_kernel, n_tiles=nt),
        grid=(b, nt),
        in_specs=[pl.BlockSpec((None, CONV_HALO, d), lambda i, t: (i, jnp.maximum(t * hb - 1, 0), 0)),
                  pl.BlockSpec((None, ts, d), lambda i, t: (i, t, 0)),
                  pl.BlockSpec((None, CONV_HALO, d), lambda i, t: (i, jnp.minimum((t + 1) * hb, n_hb - 1), 0)),
                  pl.BlockSpec((None, 3, d), lambda i, t: (i, 0, 0)),
                  _const_spec((1, d)), _const_spec(w_in.shape), _const_spec((1, 3 * d)),
                  _const_spec(dw_w.shape), _const_spec((1, d)), _const_spec((1, d)), _const_spec((1, d)),
                  _const_spec(w_out.shape), _const_spec((1, d))],
        out_specs=pl.BlockSpec((None, ts, d), lambda i, t: (i, t, 0)),
        out_shape=jax.ShapeDtypeStruct((b, s, d), F32),
        scratch_shapes=[pltpu.VMEM((d // LANES, ts + 2 * CONV_HALO, LANES), F32), pltpu.VMEM((ts, d), F32)],
        compiler_params=_cparams(("arbitrary", "arbitrary")),
        name="odd_layer",
    )(x, x, x, mod, row(pre_g), w_in, row(b_in), dw_w, row(dw_b), row(ln_g), row(ln_b), w_out, row(post_g))


def _pack_even_weights(w_in, w_uq, w_ukv, w_out):
    d = w_in.shape[0]
    hq = RET_HEADS * RET_QK_DIM
    sizes = (hq, hq, RET_WIDTH, RET_WIDTH, MLA_Q_RANK, MLA_KV_RANK, MLA_ROPE_DIM, MLA_WIDTH)
    pieces, start = [], 0
    for size in sizes:
        pieces.append(w_in[:, start:start + size])
        start += size
    pieces.insert(7, jnp.zeros((d, LANES - MLA_ROPE_DIM), w_in.dtype))
    w_in_p = jnp.concatenate(pieces, axis=1).astype(BF16)
    wq = w_uq.reshape(MLA_Q_RANK, MLA_HEADS, MLA_NOPE_DIM + MLA_ROPE_DIM)
    w_uq_p = jnp.concatenate([wq[:, :, :MLA_NOPE_DIM].reshape(MLA_Q_RANK, -1),
                              wq[:, :, MLA_NOPE_DIM:].reshape(MLA_Q_RANK, -1)], axis=1).astype(BF16)
    return w_in_p, w_uq_p, w_ukv.astype(BF16), w_out.astype(BF16)


def kernel(x, c, positions, ada_w, ada_b, pre_g, post_g, ev_w_in, ev_dec_f, ev_dec_b, ev_q_norm_g, ev_w_uq, ev_kv_norm_g, ev_w_ukv, ev_w_out, od_w_in, od_b_in, od_dw_w, od_dw_b, od_ln_g, od_ln_b, od_w_out):
    depth = ada_w.shape[0]
    s = x.shape[1]
    assert s % TOKEN_TILE == 0 and TOKEN_TILE % RET_CHUNK == 0
    mod = _modulation(c, ada_w, ada_b)
    cos_t, sin_t = _rope_tables(positions)
    for layer in range(depth):
        i = layer // 2
        if layer % 2 == 0:
            w_in_p, w_uq_p, w_ukv, w_out = _pack_even_weights(ev_w_in[i], ev_w_uq[i], ev_w_ukv[i], ev_w_out[i])
            rq, rk, rv, gates, qt, kc, vt = _even_in(x, mod[layer], pre_g[layer], w_in_p, ev_q_norm_g[i], w_uq_p,
                                                     ev_kv_norm_g[i], w_ukv, cos_t, sin_t)
            ret = _retention(rq, rk, rv, ev_dec_f[i], ev_dec_b[i])
            att = _attention(qt, kc, vt)
            x = _even_out(x, ret, att, gates, w_out, post_g[layer], mod[layer])
        else:
            x = _odd_layer(x, mod[layer], pre_g[layer], post_g[layer], od_w_in[i].astype(BF16), od_b_in[i],
                           od_dw_w[i], od_dw_b[i], od_ln_g[i], od_ln_b[i], od_w_out[i].astype(BF16))
    return x
```

```python
import functools
import math

import jax
import jax.numpy as jnp
from jax import lax
from jax.experimental import pallas as pl
from jax.experimental.pallas import tpu as pltpu

F32 = jnp.float32
BF16 = jnp.bfloat16

RET_HEADS = 4
RET_QK_DIM = 64
RET_V_DIM = 128
RET_CHUNK = 128
MLA_HEADS = 4
MLA_Q_RANK = 384
MLA_KV_RANK = 256
MLA_NOPE_DIM = 128
MLA_ROPE_DIM = 64
MLA_V_DIM = 128
ROPE_DIM = 64
ROPE_BASE = 10000.0
CONV_KERNEL = 31
EPS = 1e-6

RET_WIDTH = RET_HEADS * RET_V_DIM
MLA_WIDTH = MLA_HEADS * MLA_V_DIM
RET_SCALE = RET_QK_DIM ** -0.5
ATT_SCALE_LOG2 = (MLA_NOPE_DIM + MLA_ROPE_DIM) ** -0.5 * math.log2(math.e)

LANES = 128
MXU_DIM = 256
ATT_QK_PAD = MXU_DIM
ROPE_PACK = LANES // (ROPE_DIM // 2)
CONV_HALO = 16

TOKEN_TILE = 512
EVEN_IN_TILE = 1024
WIDE_TOKEN_TILE = 1024
ATT_Q_TILE = 1024
ATT_KV_TILE = 2 * TOKEN_TILE
ATT_PV_CHUNK = MXU_DIM
ATT_VT_ROWS = MLA_V_DIM + 16
CONV_ROW_CHUNK = 64
ODD_COL_BLOCK = MXU_DIM
RET_CHUNKS_PER_ITER = 32

V7X_VMEM_BYTES = 64 * 1024 * 1024
VMEM_LIMIT_BYTES = V7X_VMEM_BYTES - 8 * 1024 * 1024

_C_RQ = 0
_C_RK = _C_RQ + RET_HEADS * RET_QK_DIM
_C_RV = _C_RK + RET_HEADS * RET_QK_DIM
_C_RG = _C_RV + RET_WIDTH
_C_CQ = _C_RG + RET_WIDTH
_C_CKV = _C_CQ + MLA_Q_RANK
_C_KR = _C_CKV + MLA_KV_RANK
_C_MG = _C_KR + LANES
_C_END = _C_MG + MLA_WIDTH


def _cparams(semantics):
    return pltpu.CompilerParams(dimension_semantics=semantics, vmem_limit_bytes=VMEM_LIMIT_BYTES)


def _const_spec(shape):
    nd = len(shape)
    return pl.BlockSpec(shape, lambda *_: (0,) * nd, pipeline_mode=pl.Buffered(1))


def _silu(v):
    return v * jax.nn.sigmoid(v)


def _rms(v, g):
    return v * lax.rsqrt(jnp.mean(v * v, axis=-1, keepdims=True) + EPS) * g


def _mod_kernel(c_ref, w_ref, b_ref, o_ref):
    c = c_ref[...]
    o_ref[...] = jnp.dot(_silu(c).astype(BF16), w_ref[...].astype(BF16),
                         preferred_element_type=F32) + b_ref[...]


def _modulation(c, ada_w, ada_b):
    depth, d, d3 = ada_w.shape
    b = c.shape[0]
    rows = -(-b // 8) * 8
    c_pad = jnp.pad(c, ((0, rows - b), (0, 0)))
    out = pl.pallas_call(
        _mod_kernel,
        grid=(depth, d3 // d),
        in_specs=[pl.BlockSpec((rows, d), lambda l, j: (0, 0)),
                  pl.BlockSpec((None, d, d), lambda l, j: (l, 0, j)),
                  pl.BlockSpec((None, 1, d), lambda l, j: (l, 0, j))],
        out_specs=pl.BlockSpec((None, rows, d), lambda l, j: (l, 0, j)),
        out_shape=jax.ShapeDtypeStruct((depth, rows, d3), F32),
        compiler_params=_cparams(("arbitrary", "arbitrary")),
        name="adaln_mod",
    )(c_pad, ada_w, ada_b.reshape(depth, 1, d3))
    return out[:, :b].reshape(depth, b, d3 // d, d)


def _rope_table_kernel(pos_ref, invf_ref, cos_ref, sin_ref):
    rows = pos_ref.shape[0]
    half = ROPE_DIM // 2
    ang = pos_ref[...].astype(F32) * invf_ref[...]
    lane = lax.broadcasted_iota(jnp.int32, ang.shape, 1)
    for table, dst in ((jnp.cos(ang), cos_ref), (jnp.sin(ang), sin_ref)):
        for q in range(ROPE_PACK):
            v = pltpu.roll(table, LANES - half * q, 1) if q else table
            v = jnp.where(lane < half, v, 0.0)
            v = v + pltpu.roll(v, half, 1)
            v = v + pltpu.roll(v, 2 * half, 1)
            dst[pl.ds(q, rows, stride=ROPE_PACK), :] = v


def _rope_tables(positions):
    b, s = positions.shape
    half = ROPE_DIM // 2
    inv_freq = ROPE_BASE ** (-jnp.arange(0, ROPE_DIM, 2, dtype=F32) / ROPE_DIM)
    invf = jnp.tile(inv_freq, ROPE_PACK).reshape(1, LANES)
    rows = s // ROPE_PACK
    pos = jnp.repeat(positions.reshape(b, rows, ROPE_PACK), half, axis=-1)
    tr = min(TOKEN_TILE, rows)
    return pl.pallas_call(
        _rope_table_kernel,
        grid=(b, rows // tr),
        in_specs=[pl.BlockSpec((None, tr, LANES), lambda i, t: (i, t, 0)),
                  pl.BlockSpec((1, LANES), lambda i, t: (0, 0))],
        out_specs=[pl.BlockSpec((None, ROPE_PACK * tr, LANES), lambda i, t: (i, t, 0))] * 2,
        out_shape=[jax.ShapeDtypeStruct((b, s, LANES), F32)] * 2,
        compiler_params=_cparams(("arbitrary", "arbitrary")),
        name="rope_tables",
    )(pos, invf)


def _rope_slab(v, cos, sin, first_half):
    half = ROPE_DIM // 2
    partner = jnp.where(first_half, -pltpu.roll(v, LANES - half, 1), pltpu.roll(v, half, 1))
    return v * cos + partner * sin


def _even_in_kernel(x_ref, mod_ref, pg_ref, win_ref, qg_ref, wuq_ref, kvg_ref, wukv_ref,
                    cos_ref, sin_ref,
                    rq_ref, rk_ref, rv_ref, gate_ref, qt_ref, kc_ref, vt_ref):
    x = x_ref[...]
    h = (_rms(x, pg_ref[...]) * (1.0 + mod_ref[1:2, :]) + mod_ref[0:1, :]).astype(BF16)

    def proj(lo, hi):
        return jnp.dot(h, win_ref[:, lo:hi], preferred_element_type=F32)

    cos = cos_ref[...]
    sin = sin_ref[...]
    lane = lax.broadcasted_iota(jnp.int32, cos.shape, 1)
    first_half = (lane % ROPE_DIM) < (ROPE_DIM // 2)
    rope = functools.partial(_rope_slab, cos=cos, sin=sin, first_half=first_half)

    z_lat = proj(_C_CQ, _C_MG)
    cqn = _rms(z_lat[:, :MLA_Q_RANK], qg_ref[...])
    q = jnp.dot(cqn.astype(BF16), wuq_ref[...], preferred_element_type=F32)
    kvn = _rms(z_lat[:, MLA_Q_RANK:MLA_Q_RANK + MLA_KV_RANK], kvg_ref[...])
    kv = jnp.dot(kvn.astype(BF16), wukv_ref[...], preferred_element_type=F32)
    k_rope = rope(z_lat[:, _C_KR - _C_CQ:]).astype(BF16)

    z_ret = proj(_C_RQ, _C_CQ)
    for j in range(RET_HEADS * RET_QK_DIM // LANES):
        sl = slice(j * LANES, (j + 1) * LANES)
        rq_ref[:, sl] = rope(z_ret[:, _C_RQ + j * LANES:_C_RQ + (j + 1) * LANES]).astype(BF16)
        rk_ref[:, sl] = (rope(z_ret[:, _C_RK + j * LANES:_C_RK + (j + 1) * LANES]) * RET_SCALE).astype(BF16)
    rv_ref[...] = z_ret[:, _C_RV:_C_RG].astype(BF16)
    gate_ref[:, :RET_WIDTH] = _silu(z_ret[:, _C_RG:_C_CQ]).astype(BF16)
    gate_ref[:, RET_WIDTH:] = _silu(proj(_C_MG, _C_END)).astype(BF16)

    nope_w = MLA_HEADS * MLA_NOPE_DIM
    q_rope = [rope(q[:, nope_w + j * LANES:nope_w + (j + 1) * LANES]) for j in range(2)]
    low = lane < ROPE_DIM
    ones_rows = jnp.ones((ATT_VT_ROWS - MLA_V_DIM, TOKEN_TILE), BF16)
    for hd in range(MLA_HEADS):
        qn = q[:, hd * MLA_NOPE_DIM:(hd + 1) * MLA_NOPE_DIM] * ATT_SCALE_LOG2
        slab = q_rope[hd // 2]
        if hd % 2:
            slab = pltpu.roll(slab, ROPE_DIM, 1)
        qr = jnp.where(low, slab, 0.0) * ATT_SCALE_LOG2
        qt_ref[hd, :MLA_NOPE_DIM, :] = qn.T.astype(BF16)
        qt_ref[hd, MLA_NOPE_DIM:, :] = qr.T.astype(BF16)
        base = hd * (MLA_NOPE_DIM + MLA_V_DIM)
        kc_ref[hd, :, :MLA_NOPE_DIM] = kv[:, base:base + MLA_NOPE_DIM].astype(BF16)
        kc_ref[hd, :, MLA_NOPE_DIM:] = k_rope
        v_t = kv[:, base + MLA_NOPE_DIM:base + MLA_NOPE_DIM + MLA_V_DIM].T.astype(BF16)
        for blk in range(x.shape[0] // TOKEN_TILE):
            vt_ref[hd, blk, :MLA_V_DIM, :] = v_t[:, blk * TOKEN_TILE:(blk + 1) * TOKEN_TILE]
            vt_ref[hd, blk, MLA_V_DIM:, :] = ones_rows


def _even_in(x, mod, pre_g, w_in_p, q_norm_g, w_uq_p, kv_norm_g, w_ukv, cos_t, sin_t):
    b, s, d = x.shape
    ts = math.gcd(s, EVEN_IN_TILE)
    nt = s // ts
    vt_blocks = ts // TOKEN_TILE
    hq = RET_HEADS * RET_QK_DIM
    tile3 = lambda w: pl.BlockSpec((None, ts, w), lambda i, t: (i, t, 0))
    out_shapes = [
        jax.ShapeDtypeStruct((b, s, hq), BF16),
        jax.ShapeDtypeStruct((b, s, hq), BF16),
        jax.ShapeDtypeStruct((b, s, RET_WIDTH), BF16),
        jax.ShapeDtypeStruct((b, s, RET_WIDTH + MLA_WIDTH), BF16),
        jax.ShapeDtypeStruct((b, MLA_HEADS, ATT_QK_PAD, s), BF16),
        jax.ShapeDtypeStruct((b, MLA_HEADS, s, ATT_QK_PAD), BF16),
        jax.ShapeDtypeStruct((b, MLA_HEADS, s // TOKEN_TILE, ATT_VT_ROWS, TOKEN_TILE), BF16),
    ]
    out_specs = [
        tile3(hq), tile3(hq), tile3(RET_WIDTH), tile3(RET_WIDTH + MLA_WIDTH),
        pl.BlockSpec((None, MLA_HEADS, ATT_QK_PAD, ts), lambda i, t: (i, 0, 0, t)),
        pl.BlockSpec((None, MLA_HEADS, ts, ATT_QK_PAD), lambda i, t: (i, 0, t, 0)),
        pl.BlockSpec((None, MLA_HEADS, vt_blocks, ATT_VT_ROWS, TOKEN_TILE), lambda i, t: (i, 0, t, 0, 0)),
    ]
    return pl.pallas_call(
        _even_in_kernel,
        grid=(b, nt),
        in_specs=[tile3(d),
                  pl.BlockSpec((None, 3, d), lambda i, t: (i, 0, 0)),
                  _const_spec((1, d)),
                  _const_spec(w_in_p.shape),
                  _const_spec((1, MLA_Q_RANK)),
                  _const_spec(w_uq_p.shape),
                  _const_spec((1, MLA_KV_RANK)),
                  _const_spec(w_ukv.shape),
                  tile3(LANES), tile3(LANES)],
        out_specs=out_specs,
        out_shape=out_shapes,
        compiler_params=_cparams(("arbitrary", "arbitrary")),
        name="even_in",
    )(x, mod, pre_g.reshape(1, d), w_in_p, q_norm_g.reshape(1, -1), w_uq_p,
      kv_norm_g.reshape(1, -1), w_ukv, cos_t, sin_t)


def _retention_kernel(dec_ref, q_ref, k_ref, v_ref, o_ref, sf_ref, kvb_ref, *, n_chunks, per_iter):
    c = RET_CHUNK
    dk = RET_QK_DIM
    lg = jax.nn.log_sigmoid(dec_ref[...])
    ri = lax.broadcasted_iota(jnp.int32, (c, LANES), 0)
    ci = lax.broadcasted_iota(jnp.int32, (c, LANES), 1)
    rif = ri.astype(F32)
    diff = rif - ci.astype(F32)
    row_h0 = ri < dk
    lane_h0 = ci < dk

    def dmat(lgf, lgb):
        return jnp.where(diff >= 0, jnp.exp(lgf * jnp.maximum(diff, 0.0)), jnp.exp(lgb * jnp.maximum(-diff, 0.0)))

    dec = jnp.concatenate([dmat(lg[0:1], lg[2:3]), dmat(lg[1:2], lg[3:4])], axis=0)
    qwf = jnp.concatenate([jnp.exp(lg[0:1] * (rif + 1.0)), jnp.exp(lg[1:2] * (rif + 1.0))], axis=0)
    qwb = jnp.concatenate([jnp.exp(lg[2:3] * (c - rif)), jnp.exp(lg[3:4] * (c - rif))], axis=0)
    lgf_lane = jnp.where(lane_h0, lg[0:1], lg[1:2])
    lgb_lane = jnp.where(lane_h0, lg[2:3], lg[3:4])
    kwf = jnp.exp(lgf_lane * (c - 1.0 - rif))
    kwb = jnp.exp(lgb_lane * rif)
    cdf = jnp.where(row_h0, jnp.exp(lg[0:1] * c), jnp.exp(lg[1:2] * c))
    cdb = jnp.where(row_h0, jnp.exp(lg[2:3] * c), jnp.exp(lg[3:4] * c))

    def rows_of(n):
        return pl.ds(pl.multiple_of(n * c, c), c)

    def pick_heads(m):
        return jnp.where(row_h0, m[:, :RET_V_DIM], m[:, RET_V_DIM:])

    def fwd_body(n, state_f):
        rows = rows_of(n)
        k = k_ref[rows, :].astype(F32)
        kw_t = jnp.concatenate([k * kwf, k * kwb], axis=1).T.astype(BF16)
        kv = jnp.dot(kw_t, v_ref[rows, :], preferred_element_type=F32)
        sf_ref[n] = state_f.astype(BF16)
        kvb_ref[n] = pick_heads(kv[c:, :])
        return state_f * cdf + pick_heads(kv[:c, :])

    def grouped(body):
        def group(i, state):
            for u in range(per_iter):
                state = body(i * per_iter + u, state)
            return state
        return group

    lax.fori_loop(0, n_chunks // per_iter, grouped(fwd_body), jnp.zeros((c, LANES), F32))

    def norm(o):
        mu = jnp.mean(o, axis=-1, keepdims=True)
        var = jnp.mean(jnp.square(o - mu), axis=-1, keepdims=True)
        return (o - mu) * lax.rsqrt(var + EPS)

    def bwd_body(t, state_b):
        n = n_chunks - 1 - t
        rows = rows_of(n)
        q = q_ref[rows, :].astype(F32)
        qm = jnp.concatenate([jnp.where(lane_h0, q, 0.0), jnp.where(lane_h0, 0.0, q)], axis=0)
        s = lax.dot_general(qm.astype(BF16), k_ref[rows, :], (((1,), (1,)), ((), ())),
                            preferred_element_type=F32)
        p = (s * dec).astype(BF16)
        v = v_ref[rows, :]
        inner0 = jnp.dot(p[:c], v[:, :RET_V_DIM], preferred_element_type=F32)
        inner1 = jnp.dot(p[c:], v[:, RET_V_DIM:], preferred_element_type=F32)
        lhs = jnp.concatenate([qm * qwf, qm * qwb], axis=1).astype(BF16)
        rhs = jnp.concatenate([sf_ref[n], state_b.astype(BF16)], axis=0)
        cross = jnp.dot(lhs, rhs, preferred_element_type=F32)
        o_ref[rows, :RET_V_DIM] = norm(inner0 + cross[:c]).astype(BF16)
        o_ref[rows, RET_V_DIM:] = norm(inner1 + cross[c:]).astype(BF16)
        return state_b * cdb + kvb_ref[n]

    lax.fori_loop(0, n_chunks // per_iter, grouped(bwd_body), jnp.zeros((c, LANES), F32))


def _retention(rq, rk, rv, dec_f, dec_b):
    b, s, _ = rq.shape
    n_chunks = s // RET_CHUNK
    pairs = RET_HEADS // 2
    dec = jnp.concatenate([dec_f.reshape(pairs, 2), dec_b.reshape(pairs, 2)], axis=1)
    dec = jnp.broadcast_to(dec[:, :, None], (pairs, 4, LANES)).astype(F32)
    seq = lambda w: pl.BlockSpec((None, s, w), lambda i, p: (i, 0, p))
    return pl.pallas_call(
        functools.partial(_retention_kernel, n_chunks=n_chunks, per_iter=math.gcd(n_chunks, RET_CHUNKS_PER_ITER)),
        grid=(b, pairs),
        in_specs=[pl.BlockSpec((None, 4, LANES), lambda i, p: (p, 0, 0)),
                  seq(2 * RET_QK_DIM), seq(2 * RET_QK_DIM), seq(2 * RET_V_DIM)],
        out_specs=seq(2 * RET_V_DIM),
        out_shape=jax.ShapeDtypeStruct((b, s, RET_WIDTH), BF16),
        scratch_shapes=[pltpu.VMEM((n_chunks, RET_CHUNK, LANES), BF16),
                        pltpu.VMEM((n_chunks, RET_CHUNK, LANES), F32)],
        compiler_params=_cparams(("arbitrary", "arbitrary")),
        name="retention",
    )(dec, rq, rk, rv)


def _attention_kernel(qt_ref, qt_next_ref, kc_ref, vt_ref, o_ref, s_ref, *, n_kv, tk):
    qt = qt_ref[...]
    tq = qt.shape[1]
    ck = ATT_PV_CHUNK
    vt_per_kv = tk // TOKEN_TILE
    chunks_per_vt = TOKEN_TILE // ck

    def scores(j, q):
        k = kc_ref[pl.ds(pl.multiple_of(j * tk, tk), tk), :]
        return jnp.dot(k, q, preferred_element_type=F32)

    def softmax_pv(slot, j, m, acc):
        m_new = jnp.maximum(m, jnp.max(s_ref[slot], axis=0, keepdims=True))
        acc = jnp.exp2(m - m_new) * acc
        for c in range(tk // ck):
            p = jnp.exp2(s_ref[slot, c * ck:(c + 1) * ck, :] - m_new).astype(BF16)
            lo = (c % chunks_per_vt) * ck
            vt = vt_ref[j * vt_per_kv + c // chunks_per_vt, :, lo:lo + ck]
            acc = acc + jnp.dot(vt, p, preferred_element_type=F32)
        return m_new, acc

    def pair(j, m, acc, j_ahead, q_ahead):
        s_ref[1] = scores(j + 1, qt)
        m, acc = softmax_pv(0, j, m, acc)
        s_ref[0] = scores(j_ahead, q_ahead)
        return softmax_pv(1, j + 1, m, acc)

    @pl.when(pl.program_id(2) == 0)
    def _():
        s_ref[0] = scores(0, qt)

    carry = (jnp.full((1, tq), -1e30, F32), jnp.zeros((ATT_VT_ROWS, tq), F32))
    carry = lax.fori_loop(0, n_kv // 2 - 1, lambda i, c: pair(2 * i, c[0], c[1], 2 * i + 2, qt), carry)
    _, acc = pair(n_kv - 2, carry[0], carry[1], 0, qt_next_ref[...])
    o_ref[...] = (acc[:MLA_V_DIM] / acc[MLA_V_DIM:MLA_V_DIM + 1]).T.astype(BF16)


def _attention(qt, kc, vt):
    b, h, _, s = qt.shape
    n_vt = vt.shape[2]
    tq = min(ATT_Q_TILE, s)
    tk = ATT_KV_TILE if s % (2 * ATT_KV_TILE) == 0 else TOKEN_TILE
    assert s % (2 * tk) == 0 and s % tq == 0
    nq = s // tq
    return pl.pallas_call(
        functools.partial(_attention_kernel, n_kv=s // tk, tk=tk),
        grid=(b, h, nq),
        in_specs=[pl.BlockSpec((None, None, ATT_QK_PAD, tq), lambda i, j, t: (i, j, 0, t)),
                  pl.BlockSpec((None, None, ATT_QK_PAD, tq), lambda i, j, t: (i, j, 0, jnp.minimum(t + 1, nq - 1))),
                  pl.BlockSpec((None, None, s, ATT_QK_PAD), lambda i, j, t: (i, j, 0, 0)),
                  pl.BlockSpec((None, None, n_vt, ATT_VT_ROWS, TOKEN_TILE), lambda i, j, t: (i, j, 0, 0, 0))],
        out_specs=pl.BlockSpec((None, tq, MLA_V_DIM), lambda i, j, t: (i, t, j)),
        out_shape=jax.ShapeDtypeStruct((b, s, MLA_WIDTH), BF16),
        scratch_shapes=[pltpu.VMEM((2, tk, tq), F32)],
        compiler_params=_cparams(("arbitrary", "arbitrary", "arbitrary")),
        name="mla_attention",
    )(qt, qt, kc, vt)


def _even_out_kernel(x_ref, ret_ref, att_ref, gate_ref, w_ref, pg_ref, mod_ref, o_ref):
    m_ret = ret_ref[...] * gate_ref[:, :RET_WIDTH]
    m_att = att_ref[...] * gate_ref[:, RET_WIDTH:]
    y = (jnp.dot(m_ret, w_ref[:RET_WIDTH, :], preferred_element_type=F32)
         + jnp.dot(m_att, w_ref[RET_WIDTH:, :], preferred_element_type=F32))
    o_ref[...] = x_ref[...] + mod_ref[2:3, :] * _rms(y, pg_ref[...])


def _even_out(x, ret, att, gates, w_out, post_g, mod):
    b, s, d = x.shape
    ts = math.gcd(s, WIDE_TOKEN_TILE)
    tile3 = lambda w: pl.BlockSpec((None, ts, w), lambda i, t: (i, t, 0))
    return pl.pallas_call(
        _even_out_kernel,
        grid=(b, s // ts),
        in_specs=[tile3(d), tile3(RET_WIDTH), tile3(MLA_WIDTH), tile3(RET_WIDTH + MLA_WIDTH),
                  _const_spec(w_out.shape), _const_spec((1, d)),
                  pl.BlockSpec((None, 3, d), lambda i, t: (i, 0, 0))],
        out_specs=tile3(d),
        out_shape=jax.ShapeDtypeStruct((b, s, d), F32),
        compiler_params=_cparams(("arbitrary", "arbitrary")),
        name="even_out",
    )(x, ret, att, gates, w_out, post_g.reshape(1, d), mod)


def _odd_kernel(xp_ref, x_ref, xn_ref, mod_ref, pg_ref, win_ref, bin_ref, dww_ref, dwb_ref,
                lng_ref, lnb_ref, wout_ref, postg_ref, o_ref, u_ref, c_ref, *, n_tiles):
    t = pl.program_id(1)
    ts, d = x_ref.shape
    halo = CONV_HALO
    x = x_ref[...]
    xa = jnp.concatenate([xp_ref[...], x, xn_ref[...]], axis=0)
    h = (_rms(xa, pg_ref[...]) * (1.0 + mod_ref[1:2, :]) + mod_ref[0:1, :]).astype(BF16)
    row = lax.broadcasted_iota(jnp.int32, (ts + 2 * halo, 1), 0)
    inside = jnp.logical_and(jnp.logical_or(row >= halo, t > 0),
                             jnp.logical_or(row < ts + halo, t < n_tiles - 1))
    off = halo - CONV_KERNEL // 2
    gate = _silu(jnp.dot(h[halo:halo + ts], win_ref[:, 2 * d:], preferred_element_type=F32) + bin_ref[:, 2 * d:])

    for cb in range(0, d, ODD_COL_BLOCK):
        cols = slice(cb, cb + ODD_COL_BLOCK)
        gcols = slice(d + cb, d + cb + ODD_COL_BLOCK)
        za = jnp.dot(h, win_ref[:, cols], preferred_element_type=F32) + bin_ref[:, cols]
        zb = jnp.dot(h, win_ref[:, gcols], preferred_element_type=F32) + bin_ref[:, gcols]
        u = jnp.where(inside, za * jax.nn.sigmoid(zb), 0.0)
        for j in range(ODD_COL_BLOCK // LANES):
            sl = cb // LANES + j
            cs = slice(sl * LANES, (sl + 1) * LANES)
            u_ref[sl] = u[:, j * LANES:(j + 1) * LANES]
            for r0 in range(0, ts, CONV_ROW_CHUNK):
                acc = jnp.broadcast_to(dwb_ref[:, cs], (CONV_ROW_CHUNK, LANES))
                for tap in range(CONV_KERNEL):
                    acc = acc + u_ref[sl, pl.ds(r0 + off + tap, CONV_ROW_CHUNK), :] * dww_ref[tap:tap + 1, cs]
                c_ref[pl.ds(r0, CONV_ROW_CHUNK), cs] = acc

    cv = c_ref[...]
    mu = jnp.mean(cv, axis=-1, keepdims=True)
    var = jnp.mean(jnp.square(cv - mu), axis=-1, keepdims=True)
    ln = (cv - mu) * lax.rsqrt(var + EPS) * lng_ref[...] + lnb_ref[...]
    m = (_silu(ln) * gate).astype(BF16)
    y = jnp.dot(m, wout_ref[...], preferred_element_type=F32)
    o_ref[...] = x + mod_ref[2:3, :] * _rms(y, postg_ref[...])


def _odd_layer(x, mod, pre_g, post_g, w_in, b_in, dw_w, dw_b, ln_g, ln_b, w_out):
    b, s, d = x.shape
    ts = math.gcd(s, WIDE_TOKEN_TILE)
    nt = s // ts
    hb = ts // CONV_HALO
    n_hb = s // CONV_HALO
    row = lambda v: v.reshape(1, -1)
    return pl.pallas_call(
        functools.partial(_odd_kernel, n_tiles=nt),
        grid=(b, nt),
        in_specs=[pl.BlockSpec((None, CONV_HALO, d), lambda i, t: (i, jnp.maximum(t * hb - 1, 0), 0)),
                  pl.BlockSpec((None, ts, d), lambda i, t: (i, t, 0)),
                  pl.BlockSpec((None, CONV_HALO, d), lambda i, t: (i, jnp.minimum((t + 1) * hb, n_hb - 1), 0)),
                  pl.BlockSpec((None, 3, d), lambda i, t: (i, 0, 0)),
                  _const_spec((1, d)), _const_spec(w_in.shape), _const_spec((1, 3 * d)),
                  _const_spec(dw_w.shape), _const_spec((1, d)), _const_spec((1, d)), _const_spec((1, d)),
                  _const_spec(w_out.shape), _const_spec((1, d))],
        out_specs=pl.BlockSpec((None, ts, d), lambda i, t: (i, t, 0)),
        out_shape=jax.ShapeDtypeStruct((b, s, d), F32),
        scratch_shapes=[pltpu.VMEM((d // LANES, ts + 2 * CONV_HALO, LANES), F32), pltpu.VMEM((ts, d), F32)],
        compiler_params=_cparams(("arbitrary", "arbitrary")),
        name="odd_layer",
    )(x, x, x, mod, row(pre_g), w_in, row(b_in), dw_w, row(dw_b), row(ln_g), row(ln_b), w_out, row(post_g))


def _pack_even_weights(w_in, w_uq, w_ukv, w_out):
    d = w_in.shape[0]
    hq = RET_HEADS * RET_QK_DIM
    sizes = (hq, hq, RET_WIDTH, RET_WIDTH, MLA_Q_RANK, MLA_KV_RANK, MLA_ROPE_DIM, MLA_WIDTH)
    pieces, start = [], 0
    for size in sizes:
        pieces.append(w_in[:, start:start + size])
        start += size
    pieces.insert(7, jnp.zeros((d, LANES - MLA_ROPE_DIM), w_in.dtype))
    w_in_p = jnp.concatenate(pieces, axis=1).astype(BF16)
    wq = w_uq.reshape(MLA_Q_RANK, MLA_HEADS, MLA_NOPE_DIM + MLA_ROPE_DIM)
    w_uq_p = jnp.concatenate([wq[:, :, :MLA_NOPE_DIM].reshape(MLA_Q_RANK, -1),
                              wq[:, :, MLA_NOPE_DIM:].reshape(MLA_Q_RANK, -1)], axis=1).astype(BF16)
    return w_in_p, w_uq_p, w_ukv.astype(BF16), w_out.astype(BF16)


def kernel(x, c, positions, ada_w, ada_b, pre_g, post_g, ev_w_in, ev_dec_f, ev_dec_b, ev_q_norm_g, ev_w_uq, ev_kv_norm_g, ev_w_ukv, ev_w_out, od_w_in, od_b_in, od_dw_w, od_dw_b, od_ln_g, od_ln_b, od_w_out):
    depth = ada_w.shape[0]
    s = x.shape[1]
    assert s % TOKEN_TILE == 0 and TOKEN_TILE % RET_CHUNK == 0
    mod = _modulation(c, ada_w, ada_b)
    cos_t, sin_t = _rope_tables(positions)
    for layer in range(depth):
        i = layer // 2
        if layer % 2 == 0:
            w_in_p, w_uq_p, w_ukv, w_out = _pack_even_weights(ev_w_in[i], ev_w_uq[i], ev_w_ukv[i], ev_w_out[i])
            rq, rk, rv, gates, qt, kc, vt = _even_in(x, mod[layer], pre_g[layer], w_in_p, ev_q_norm_g[i], w_uq_p,
                                                     ev_kv_norm_g[i], w_ukv, cos_t, sin_t)
            ret = _retention(rq, rk, rv, ev_dec_f[i], ev_dec_b[i])
            att = _attention(qt, kc, vt)
            x = _even_out(x, ret, att, gates, w_out, post_g[layer], mod[layer])
        else:
            x = _odd_layer(x, mod[layer], pre_g[layer], post_g[layer], od_w_in[i].astype(BF16), od_b_in[i],
                           od_dw_w[i], od_dw_b[i], od_ln_g[i], od_ln_b[i], od_w_out[i].astype(BF16))
    return x
```

```python
import functools
import math

import jax
import jax.numpy as jnp
from jax import lax
from jax.experimental import pallas as pl
from jax.experimental.pallas import tpu as pltpu

F32 = jnp.float32
BF16 = jnp.bfloat16

RET_HEADS = 4
RET_QK_DIM = 64
RET_V_DIM = 128
RET_CHUNK = 128
MLA_HEADS = 4
MLA_Q_RANK = 384
MLA_KV_RANK = 256
MLA_NOPE_DIM = 128
MLA_ROPE_DIM = 64
MLA_V_DIM = 128
ROPE_DIM = 64
ROPE_BASE = 10000.0
CONV_KERNEL = 31
EPS = 1e-6

RET_WIDTH = RET_HEADS * RET_V_DIM
MLA_WIDTH = MLA_HEADS * MLA_V_DIM
RET_SCALE = RET_QK_DIM ** -0.5
LOG2E = math.log2(math.e)
ATT_SCALE_LOG2 = (MLA_NOPE_DIM + MLA_ROPE_DIM) ** -0.5 * LOG2E

LANES = 128
MXU_DIM = 256
ATT_QK_PAD = MXU_DIM
ROPE_PACK = LANES // (ROPE_DIM // 2)
CONV_HALO = 16

TOKEN_TILE = 512
EVEN_IN_TILE = 1024
WIDE_TOKEN_TILE = 1024
ATT_Q_TILE = 1024
ATT_KV_TILE = 2 * TOKEN_TILE
ATT_PV_CHUNK = MXU_DIM
BF16_SUBLANE_TILE = 16
ATT_VT_ROWS = MLA_V_DIM + BF16_SUBLANE_TILE
ATT_MAX_INIT = -1e30
CONV_ROW_CHUNK = 128
ODD_COL_BLOCK = MXU_DIM
RET_CHUNKS_PER_ITER = 32

V7X_VMEM_BYTES = 64 * 1024 * 1024
VMEM_LIMIT_BYTES = V7X_VMEM_BYTES - 8 * 1024 * 1024

_C_RQ = 0
_C_RK = _C_RQ + RET_HEADS * RET_QK_DIM
_C_RV = _C_RK + RET_HEADS * RET_QK_DIM
_C_RG = _C_RV + RET_WIDTH
_C_CQ = _C_RG + RET_WIDTH
_C_CKV = _C_CQ + MLA_Q_RANK
_C_KR = _C_CKV + MLA_KV_RANK
_C_MG = _C_KR + LANES
_C_END = _C_MG + MLA_WIDTH


def _cparams(semantics):
    return pltpu.CompilerParams(dimension_semantics=semantics, vmem_limit_bytes=VMEM_LIMIT_BYTES)


def _const_spec(shape):
    nd = len(shape)
    return pl.BlockSpec(shape, lambda *_: (0,) * nd, pipeline_mode=pl.Buffered(1))


def _silu(v):
    return v * jax.nn.sigmoid(v)


def _rms(v, g):
    return v * lax.rsqrt(jnp.mean(v * v, axis=-1, keepdims=True) + EPS) * g


def _mod_kernel(c_ref, w_ref, b_ref, o_ref):
    c = c_ref[...]
    o_ref[...] = jnp.dot(_silu(c).astype(BF16), w_ref[...].astype(BF16),
                         preferred_element_type=F32) + b_ref[...]


def _modulation(c, ada_w, ada_b):
    depth, d, d3 = ada_w.shape
    b = c.shape[0]
    rows = -(-b // 8) * 8
    c_pad = jnp.pad(c, ((0, rows - b), (0, 0)))
    out = pl.pallas_call(
        _mod_kernel,
        grid=(depth, d3 // d),
        in_specs=[pl.BlockSpec((rows, d), lambda l, j: (0, 0)),
                  pl.BlockSpec((None, d, d), lambda l, j: (l, 0, j)),
                  pl.BlockSpec((None, 1, d), lambda l, j: (l, 0, j))],
        out_specs=pl.BlockSpec((None, rows, d), lambda l, j: (l, 0, j)),
        out_shape=jax.ShapeDtypeStruct((depth, rows, d3), F32),
        compiler_params=_cparams(("arbitrary", "arbitrary")),
        name="adaln_mod",
    )(c_pad, ada_w, ada_b.reshape(depth, 1, d3))
    return out[:, :b].reshape(depth, b, d3 // d, d)


def _rope_table_kernel(pos_ref, invf_ref, cos_ref, sin_ref):
    rows = pos_ref.shape[0]
    half = ROPE_DIM // 2
    ang = pos_ref[...].astype(F32) * invf_ref[...]
    lane = lax.broadcasted_iota(jnp.int32, ang.shape, 1)
    for table, dst in ((jnp.cos(ang), cos_ref), (jnp.sin(ang), sin_ref)):
        for q in range(ROPE_PACK):
            v = pltpu.roll(table, LANES - half * q, 1) if q else table
            v = jnp.where(lane < half, v, 0.0)
            v = v + pltpu.roll(v, half, 1)
            v = v + pltpu.roll(v, 2 * half, 1)
            dst[pl.ds(q, rows, stride=ROPE_PACK), :] = v


def _rope_tables(positions):
    b, s = positions.shape
    half = ROPE_DIM // 2
    inv_freq = ROPE_BASE ** (-jnp.arange(0, ROPE_DIM, 2, dtype=F32) / ROPE_DIM)
    invf = jnp.tile(inv_freq, ROPE_PACK).reshape(1, LANES)
    rows = s // ROPE_PACK
    pos = jnp.repeat(positions.reshape(b, rows, ROPE_PACK), half, axis=-1)
    tr = min(TOKEN_TILE, rows)
    return pl.pallas_call(
        _rope_table_kernel,
        grid=(b, rows // tr),
        in_specs=[pl.BlockSpec((None, tr, LANES), lambda i, t: (i, t, 0)),
                  pl.BlockSpec((1, LANES), lambda i, t: (0, 0))],
        out_specs=[pl.BlockSpec((None, ROPE_PACK * tr, LANES), lambda i, t: (i, t, 0))] * 2,
        out_shape=[jax.ShapeDtypeStruct((b, s, LANES), F32)] * 2,
        compiler_params=_cparams(("arbitrary", "arbitrary")),
        name="rope_tables",
    )(pos, invf)


def _rope_slab(v, cos, sin, first_half):
    half = ROPE_DIM // 2
    partner = jnp.where(first_half, -pltpu.roll(v, LANES - half, 1), pltpu.roll(v, half, 1))
    return v * cos + partner * sin


def _even_in_kernel(x_ref, mod_ref, pg_ref, win_ref, qg_ref, wuq_ref, kvg_ref, wukv_ref,
                    cos_ref, sin_ref,
                    rq_ref, rk_ref, rv_ref, gate_ref, qt_ref, kc_ref, vt_ref):
    x = x_ref[...]
    h = (_rms(x, pg_ref[...]) * (1.0 + mod_ref[1:2, :]) + mod_ref[0:1, :]).astype(BF16)

    def proj(lo, hi):
        return jnp.dot(h, win_ref[:, lo:hi], preferred_element_type=F32)

    cos = cos_ref[...]
    sin = sin_ref[...]
    lane = lax.broadcasted_iota(jnp.int32, cos.shape, 1)
    first_half = (lane % ROPE_DIM) < (ROPE_DIM // 2)
    rope = functools.partial(_rope_slab, cos=cos, sin=sin, first_half=first_half)

    z_lat = proj(_C_CQ, _C_MG)
    cqn = _rms(z_lat[:, :MLA_Q_RANK], qg_ref[...])
    q = jnp.dot(cqn.astype(BF16), wuq_ref[...], preferred_element_type=F32)
    kvn = _rms(z_lat[:, MLA_Q_RANK:MLA_Q_RANK + MLA_KV_RANK], kvg_ref[...])
    kv = jnp.dot(kvn.astype(BF16), wukv_ref[...], preferred_element_type=F32)
    k_rope = rope(z_lat[:, _C_KR - _C_CQ:]).astype(BF16)

    z_ret = proj(_C_RQ, _C_CQ)
    for j in range(RET_HEADS * RET_QK_DIM // LANES):
        sl = slice(j * LANES, (j + 1) * LANES)
        rq_ref[:, sl] = rope(z_ret[:, _C_RQ + j * LANES:_C_RQ + (j + 1) * LANES]).astype(BF16)
        rk_ref[:, sl] = (rope(z_ret[:, _C_RK + j * LANES:_C_RK + (j + 1) * LANES]) * RET_SCALE).astype(BF16)
    rv_ref[...] = z_ret[:, _C_RV:_C_RG].astype(BF16)
    gate_ref[:, :RET_WIDTH] = _silu(z_ret[:, _C_RG:_C_CQ]).astype(BF16)
    gate_ref[:, RET_WIDTH:] = _silu(proj(_C_MG, _C_END)).astype(BF16)

    nope_w = MLA_HEADS * MLA_NOPE_DIM
    q_rope = [rope(q[:, nope_w + j * LANES:nope_w + (j + 1) * LANES]) for j in range(2)]
    low = lane < ROPE_DIM
    ones_rows = jnp.ones((ATT_VT_ROWS - MLA_V_DIM, TOKEN_TILE), BF16)
    for hd in range(MLA_HEADS):
        qn = q[:, hd * MLA_NOPE_DIM:(hd + 1) * MLA_NOPE_DIM] * ATT_SCALE_LOG2
        slab = q_rope[hd // 2]
        if hd % 2:
            slab = pltpu.roll(slab, ROPE_DIM, 1)
        qr = jnp.where(low, slab, 0.0) * ATT_SCALE_LOG2
        qt_ref[hd, :MLA_NOPE_DIM, :] = qn.T.astype(BF16)
        qt_ref[hd, MLA_NOPE_DIM:, :] = qr.T.astype(BF16)
        base = hd * (MLA_NOPE_DIM + MLA_V_DIM)
        kc_ref[hd, :, :MLA_NOPE_DIM] = kv[:, base:base + MLA_NOPE_DIM].astype(BF16)
        kc_ref[hd, :, MLA_NOPE_DIM:] = k_rope
        v_t = kv[:, base + MLA_NOPE_DIM:base + MLA_NOPE_DIM + MLA_V_DIM].T.astype(BF16)
        for blk in range(x.shape[0] // TOKEN_TILE):
            vt_ref[hd, blk, :MLA_V_DIM, :] = v_t[:, blk * TOKEN_TILE:(blk + 1) * TOKEN_TILE]
            vt_ref[hd, blk, MLA_V_DIM:, :] = ones_rows


def _even_in(x, mod, pre_g, w_in_p, q_norm_g, w_uq_p, kv_norm_g, w_ukv, cos_t, sin_t):
    b, s, d = x.shape
    ts = math.gcd(s, EVEN_IN_TILE)
    nt = s // ts
    vt_blocks = ts // TOKEN_TILE
    hq = RET_HEADS * RET_QK_DIM
    tile3 = lambda w: pl.BlockSpec((None, ts, w), lambda i, t: (i, t, 0))
    out_shapes = [
        jax.ShapeDtypeStruct((b, s, hq), BF16),
        jax.ShapeDtypeStruct((b, s, hq), BF16),
        jax.ShapeDtypeStruct((b, s, RET_WIDTH), BF16),
        jax.ShapeDtypeStruct((b, s, RET_WIDTH + MLA_WIDTH), BF16),
        jax.ShapeDtypeStruct((b, MLA_HEADS, ATT_QK_PAD, s), BF16),
        jax.ShapeDtypeStruct((b, MLA_HEADS, s, ATT_QK_PAD), BF16),
        jax.ShapeDtypeStruct((b, MLA_HEADS, s // TOKEN_TILE, ATT_VT_ROWS, TOKEN_TILE), BF16),
    ]
    out_specs = [
        tile3(hq), tile3(hq), tile3(RET_WIDTH), tile3(RET_WIDTH + MLA_WIDTH),
        pl.BlockSpec((None, MLA_HEADS, ATT_QK_PAD, ts), lambda i, t: (i, 0, 0, t)),
        pl.BlockSpec((None, MLA_HEADS, ts, ATT_QK_PAD), lambda i, t: (i, 0, t, 0)),
        pl.BlockSpec((None, MLA_HEADS, vt_blocks, ATT_VT_ROWS, TOKEN_TILE), lambda i, t: (i, 0, t, 0, 0)),
    ]
    return pl.pallas_call(
        _even_in_kernel,
        grid=(b, nt),
        in_specs=[tile3(d),
                  pl.BlockSpec((None, 3, d), lambda i, t: (i, 0, 0)),
                  _const_spec((1, d)),
                  _const_spec(w_in_p.shape),
                  _const_spec((1, MLA_Q_RANK)),
                  _const_spec(w_uq_p.shape),
                  _const_spec((1, MLA_KV_RANK)),
                  _const_spec(w_ukv.shape),
                  tile3(LANES), tile3(LANES)],
        out_specs=out_specs,
        out_shape=out_shapes,
        compiler_params=_cparams(("arbitrary", "arbitrary")),
        name="even_in",
    )(x, mod, pre_g.reshape(1, d), w_in_p, q_norm_g.reshape(1, -1), w_uq_p,
      kv_norm_g.reshape(1, -1), w_ukv, cos_t, sin_t)


def _retention_kernel(dec_ref, q_ref, k_ref, v_ref, o_ref, sf_ref, kvb_ref, *, n_chunks, per_iter):
    c = RET_CHUNK
    dk = RET_QK_DIM
    lg = jax.nn.log_sigmoid(dec_ref[...])
    ri = lax.broadcasted_iota(jnp.int32, (c, LANES), 0)
    ci = lax.broadcasted_iota(jnp.int32, (c, LANES), 1)
    rif = ri.astype(F32)
    diff = rif - ci.astype(F32)
    row_h0 = ri < dk
    lane_h0 = ci < dk

    def dmat(lgf, lgb):
        return jnp.where(diff >= 0, jnp.exp(lgf * jnp.maximum(diff, 0.0)), jnp.exp(lgb * jnp.maximum(-diff, 0.0)))

    dec = jnp.concatenate([dmat(lg[0:1], lg[2:3]), dmat(lg[1:2], lg[3:4])], axis=0)
    qwf = jnp.concatenate([jnp.exp(lg[0:1] * (rif + 1.0)), jnp.exp(lg[1:2] * (rif + 1.0))], axis=0)
    qwb = jnp.concatenate([jnp.exp(lg[2:3] * (c - rif)), jnp.exp(lg[3:4] * (c - rif))], axis=0)
    lgf_lane = jnp.where(lane_h0, lg[0:1], lg[1:2])
    lgb_lane = jnp.where(lane_h0, lg[2:3], lg[3:4])
    kwf = jnp.exp(lgf_lane * (c - 1.0 - rif))
    kwb = jnp.exp(lgb_lane * rif)
    cdf = jnp.where(row_h0, jnp.exp(lg[0:1] * c), jnp.exp(lg[1:2] * c))
    cdb = jnp.where(row_h0, jnp.exp(lg[2:3] * c), jnp.exp(lg[3:4] * c))

    def rows_of(n):
        return pl.ds(pl.multiple_of(n * c, c), c)

    def pick_heads(m):
        return jnp.where(row_h0, m[:, :RET_V_DIM], m[:, RET_V_DIM:])

    def fwd_body(n, state_f):
        rows = rows_of(n)
        k = k_ref[rows, :].astype(F32)
        kw_t = jnp.concatenate([k * kwf, k * kwb], axis=1).T.astype(BF16)
        kv = jnp.dot(kw_t, v_ref[rows, :], preferred_element_type=F32)
        sf_ref[n] = state_f.astype(BF16)
        kvb_ref[n] = pick_heads(kv[c:, :])
        return state_f * cdf + pick_heads(kv[:c, :])

    def grouped(body):
        def group(i, state):
            for u in range(per_iter):
                state = body(i * per_iter + u, state)
            return state
        return group

    lax.fori_loop(0, n_chunks // per_iter, grouped(fwd_body), jnp.zeros((c, LANES), F32))

    def norm(o):
        mu = jnp.mean(o, axis=-1, keepdims=True)
        var = jnp.mean(jnp.square(o - mu), axis=-1, keepdims=True)
        return (o - mu) * lax.rsqrt(var + EPS)

    def bwd_body(t, state_b):
        n = n_chunks - 1 - t
        rows = rows_of(n)
        q = q_ref[rows, :].astype(F32)
        qm = jnp.concatenate([jnp.where(lane_h0, q, 0.0), jnp.where(lane_h0, 0.0, q)], axis=0)
        s = lax.dot_general(qm.astype(BF16), k_ref[rows, :], (((1,), (1,)), ((), ())),
                            preferred_element_type=F32)
        p = (s * dec).astype(BF16)
        v = v_ref[rows, :]
        inner0 = jnp.dot(p[:c], v[:, :RET_V_DIM], preferred_element_type=F32)
        inner1 = jnp.dot(p[c:], v[:, RET_V_DIM:], preferred_element_type=F32)
        lhs = jnp.concatenate([qm * qwf, qm * qwb], axis=1).astype(BF16)
        rhs = jnp.concatenate([sf_ref[n], state_b.astype(BF16)], axis=0)
        cross = jnp.dot(lhs, rhs, preferred_element_type=F32)
        o_ref[rows, :RET_V_DIM] = norm(inner0 + cross[:c]).astype(BF16)
        o_ref[rows, RET_V_DIM:] = norm(inner1 + cross[c:]).astype(BF16)
        return state_b * cdb + kvb_ref[n]

    lax.fori_loop(0, n_chunks // per_iter, grouped(bwd_body), jnp.zeros((c, LANES), F32))


def _retention(rq, rk, rv, dec_f, dec_b):
    b, s, _ = rq.shape
    n_chunks = s // RET_CHUNK
    pairs = RET_HEADS // 2
    dec = jnp.concatenate([dec_f.reshape(pairs, 2), dec_b.reshape(pairs, 2)], axis=1)
    dec = jnp.broadcast_to(dec[:, :, None], (pairs, 4, LANES)).astype(F32)
    seq = lambda w: pl.BlockSpec((None, s, w), lambda i, p: (i, 0, p))
    return pl.pallas_call(
        functools.partial(_retention_kernel, n_chunks=n_chunks, per_iter=math.gcd(n_chunks, RET_CHUNKS_PER_ITER)),
        grid=(b, pairs),
        in_specs=[pl.BlockSpec((None, 4, LANES), lambda i, p: (p, 0, 0)),
                  seq(2 * RET_QK_DIM), seq(2 * RET_QK_DIM), seq(2 * RET_V_DIM)],
        out_specs=seq(2 * RET_V_DIM),
        out_shape=jax.ShapeDtypeStruct((b, s, RET_WIDTH), BF16),
        scratch_shapes=[pltpu.VMEM((n_chunks, RET_CHUNK, LANES), BF16),
                        pltpu.VMEM((n_chunks, RET_CHUNK, LANES), F32)],
        compiler_params=_cparams(("arbitrary", "arbitrary")),
        name="retention",
    )(dec, rq, rk, rv)


def _attention_kernel(qt_ref, qt_next_ref, kc_ref, vt_ref, o_ref, s_ref, *, n_kv, tk):
    qt = qt_ref[...]
    tq = qt.shape[1]
    ck = ATT_PV_CHUNK
    vt_per_kv = tk // TOKEN_TILE
    chunks_per_vt = TOKEN_TILE // ck

    def scores(j, q):
        k = kc_ref[pl.ds(pl.multiple_of(j * tk, tk), tk), :]
        return jnp.dot(k, q, preferred_element_type=F32)

    def softmax_pv(slot, j, m, acc):
        m_new = jnp.maximum(m, jnp.max(s_ref[slot], axis=0, keepdims=True))
        acc = jnp.exp2(m - m_new) * acc
        for c in range(tk // ck):
            p = jnp.exp2(s_ref[slot, c * ck:(c + 1) * ck, :] - m_new).astype(BF16)
            lo = (c % chunks_per_vt) * ck
            vt = vt_ref[j * vt_per_kv + c // chunks_per_vt, :, lo:lo + ck]
            acc = acc + jnp.dot(vt, p, preferred_element_type=F32)
        return m_new, acc

    def pair(j, m, acc, j_ahead, q_ahead):
        s_ref[1] = scores(j + 1, qt)
        m, acc = softmax_pv(0, j, m, acc)
        s_ref[0] = scores(j_ahead, q_ahead)
        return softmax_pv(1, j + 1, m, acc)

    @pl.when(pl.program_id(2) == 0)
    def _():
        s_ref[0] = scores(0, qt)

    carry = (jnp.full((1, tq), ATT_MAX_INIT, F32), jnp.zeros((ATT_VT_ROWS, tq), F32))
    carry = lax.fori_loop(0, n_kv // 2 - 1, lambda i, c: pair(2 * i, c[0], c[1], 2 * i + 2, qt), carry)
    _, acc = pair(n_kv - 2, carry[0], carry[1], 0, qt_next_ref[...])
    o_ref[...] = (acc[:MLA_V_DIM] / acc[MLA_V_DIM:MLA_V_DIM + 1]).T.astype(BF16)


def _attention(qt, kc, vt):
    b, h, _, s = qt.shape
    n_vt = vt.shape[2]
    tq = min(ATT_Q_TILE, s)
    tk = ATT_KV_TILE if s % (2 * ATT_KV_TILE) == 0 else TOKEN_TILE
    assert s % (2 * tk) == 0 and s % tq == 0
    nq = s // tq
    return pl.pallas_call(
        functools.partial(_attention_kernel, n_kv=s // tk, tk=tk),
        grid=(b, h, nq),
        in_specs=[pl.BlockSpec((None, None, ATT_QK_PAD, tq), lambda i, j, t: (i, j, 0, t)),
                  pl.BlockSpec((None, None, ATT_QK_PAD, tq), lambda i, j, t: (i, j, 0, jnp.minimum(t + 1, nq - 1))),
                  pl.BlockSpec((None, None, s, ATT_QK_PAD), lambda i, j, t: (i, j, 0, 0)),
                  pl.BlockSpec((None, None, n_vt, ATT_VT_ROWS, TOKEN_TILE), lambda i, j, t: (i, j, 0, 0, 0))],
        out_specs=pl.BlockSpec((None, tq, MLA_V_DIM), lambda i, j, t: (i, t, j)),
        out_shape=jax.ShapeDtypeStruct((b, s, MLA_WIDTH), BF16),
        scratch_shapes=[pltpu.VMEM((2, tk, tq), F32)],
        compiler_params=_cparams(("arbitrary", "arbitrary", "arbitrary")),
        name="mla_attention",
    )(qt, qt, kc, vt)


def _even_out_kernel(x_ref, ret_ref, att_ref, gate_ref, w_ref, pg_ref, mod_ref, o_ref):
    m_ret = ret_ref[...] * gate_ref[:, :RET_WIDTH]
    m_att = att_ref[...] * gate_ref[:, RET_WIDTH:]
    y = (jnp.dot(m_ret, w_ref[:RET_WIDTH, :], preferred_element_type=F32)
         + jnp.dot(m_att, w_ref[RET_WIDTH:, :], preferred_element_type=F32))
    o_ref[...] = x_ref[...] + mod_ref[2:3, :] * _rms(y, pg_ref[...])


def _even_out(x, ret, att, gates, w_out, post_g, mod):
    b, s, d = x.shape
    ts = math.gcd(s, WIDE_TOKEN_TILE)
    tile3 = lambda w: pl.BlockSpec((None, ts, w), lambda i, t: (i, t, 0))
    return pl.pallas_call(
        _even_out_kernel,
        grid=(b, s // ts),
        in_specs=[tile3(d), tile3(RET_WIDTH), tile3(MLA_WIDTH), tile3(RET_WIDTH + MLA_WIDTH),
                  _const_spec(w_out.shape), _const_spec((1, d)),
                  pl.BlockSpec((None, 3, d), lambda i, t: (i, 0, 0))],
        out_specs=tile3(d),
        out_shape=jax.ShapeDtypeStruct((b, s, d), F32),
        compiler_params=_cparams(("arbitrary", "arbitrary")),
        name="even_out",
    )(x, ret, att, gates, w_out, post_g.reshape(1, d), mod)


def _odd_kernel(xp_ref, x_ref, xn_ref, mod_ref, pg_ref, win_ref, bin_ref, dww_ref, dwb_ref,
                lng_ref, lnb_ref, wout_ref, postg_ref, o_ref, u_ref, c_ref, *, n_tiles):
    t = pl.program_id(1)
    ts, d = x_ref.shape
    halo = CONV_HALO
    x = x_ref[...]
    xa = jnp.concatenate([xp_ref[...], x, xn_ref[...]], axis=0)
    h = (_rms(xa, pg_ref[...]) * (1.0 + mod_ref[1:2, :]) + mod_ref[0:1, :]).astype(BF16)
    row = lax.broadcasted_iota(jnp.int32, (ts + 2 * halo, 1), 0)
    inside = jnp.logical_and(jnp.logical_or(row >= halo, t > 0),
                             jnp.logical_or(row < ts + halo, t < n_tiles - 1))
    off = halo - CONV_KERNEL // 2
    gate = _silu(jnp.dot(h[halo:halo + ts], win_ref[:, 2 * d:], preferred_element_type=F32) + bin_ref[:, 2 * d:])

    for cb in range(0, d, ODD_COL_BLOCK):
        cols = slice(cb, cb + ODD_COL_BLOCK)
        gcols = slice(d + cb, d + cb + ODD_COL_BLOCK)
        za = jnp.dot(h, win_ref[:, cols], preferred_element_type=F32) + bin_ref[:, cols]
        zb = jnp.dot(h, win_ref[:, gcols], preferred_element_type=F32) + bin_ref[:, gcols]
        u = jnp.where(inside, za * jax.nn.sigmoid(zb), 0.0)
        for j in range(ODD_COL_BLOCK // LANES):
            sl = cb // LANES + j
            cs = slice(sl * LANES, (sl + 1) * LANES)
            u_ref[sl] = u[:, j * LANES:(j + 1) * LANES]
            for r0 in range(0, ts, CONV_ROW_CHUNK):
                acc = jnp.broadcast_to(dwb_ref[:, cs], (CONV_ROW_CHUNK, LANES))
                for tap in range(CONV_KERNEL):
                    acc = acc + u_ref[sl, pl.ds(r0 + off + tap, CONV_ROW_CHUNK), :] * dww_ref[tap:tap + 1, cs]
                c_ref[pl.ds(r0, CONV_ROW_CHUNK), cs] = acc

    cv = c_ref[...]
    mu = jnp.mean(cv, axis=-1, keepdims=True)
    var = jnp.mean(jnp.square(cv - mu), axis=-1, keepdims=True)
    ln = (cv - mu) * lax.rsqrt(var + EPS) * lng_ref[...] + lnb_ref[...]
    m = (_silu(ln) * gate).astype(BF16)
    y = jnp.dot(m, wout_ref[...], preferred_element_type=F32)
    o_ref[...] = x + mod_ref[2:3, :] * _rms(y, postg_ref[...])


def _odd_layer(x, mod, pre_g, post_g, w_in, b_in, dw_w, dw_b, ln_g, ln_b, w_out):
    b, s, d = x.shape
    ts = math.gcd(s, WIDE_TOKEN_TILE)
    nt = s // ts
    hb = ts // CONV_HALO
    n_hb = s // CONV_HALO
    row = lambda v: v.reshape(1, -1)
    return pl.pallas_call(
        functools.partial(_odd_kernel, n_tiles=nt),
        grid=(b, nt),
        in_specs=[pl.BlockSpec((None, CONV_HALO, d), lambda i, t: (i, jnp.maximum(t * hb - 1, 0), 0)),
                  pl.BlockSpec((None, ts, d), lambda i, t: (i, t, 0)),
                  pl.BlockSpec((None, CONV_HALO, d), lambda i, t: (i, jnp.minimum((t + 1) * hb, n_hb - 1), 0)),
                  pl.BlockSpec((None, 3, d), lambda i, t: (i, 0, 0)),
                  _const_spec((1, d)), _const_spec(w_in.shape), _const_spec((1, 3 * d)),
                  _const_spec(dw_w.shape), _const_spec((1, d)), _const_spec((1, d)), _const_spec((1, d)),
                  _const_spec(w_out.shape), _const_spec((1, d))],
        out_specs=pl.BlockSpec((None, ts, d), lambda i, t: (i, t, 0)),
        out_shape=jax.ShapeDtypeStruct((b, s, d), F32),
        scratch_shapes=[pltpu.VMEM((d // LANES, ts + 2 * CONV_HALO, LANES), F32), pltpu.VMEM((ts, d), F32)],
        compiler_params=_cparams(("arbitrary", "arbitrary")),
        name="odd_layer",
    )(x, x, x, mod, row(pre_g), w_in, row(b_in), dw_w, row(dw_b), row(ln_g), row(ln_b), w_out, row(post_g))


def _pack_even_weights(w_in, w_uq, w_ukv, w_out):
    d = w_in.shape[0]
    hq = RET_HEADS * RET_QK_DIM
    sizes = (hq, hq, RET_WIDTH, RET_WIDTH, MLA_Q_RANK, MLA_KV_RANK, MLA_ROPE_DIM, MLA_WIDTH)
    pieces, start = [], 0
    for size in sizes:
        pieces.append(w_in[:, start:start + size])
        start += size
    pieces.insert(7, jnp.zeros((d, LANES - MLA_ROPE_DIM), w_in.dtype))
    w_in_p = jnp.concatenate(pieces, axis=1).astype(BF16)
    wq = w_uq.reshape(MLA_Q_RANK, MLA_HEADS, MLA_NOPE_DIM + MLA_ROPE_DIM)
    w_uq_p = jnp.concatenate([wq[:, :, :MLA_NOPE_DIM].reshape(MLA_Q_RANK, -1),
                              wq[:, :, MLA_NOPE_DIM:].reshape(MLA_Q_RANK, -1)], axis=1).astype(BF16)
    return w_in_p, w_uq_p, w_ukv.astype(BF16), w_out.astype(BF16)


def kernel(x, c, positions, ada_w, ada_b, pre_g, post_g, ev_w_in, ev_dec_f, ev_dec_b, ev_q_norm_g, ev_w_uq, ev_kv_norm_g, ev_w_ukv, ev_w_out, od_w_in, od_b_in, od_dw_w, od_dw_b, od_ln_g, od_ln_b, od_w_out):
    depth = ada_w.shape[0]
    s = x.shape[1]
    assert s % TOKEN_TILE == 0 and TOKEN_TILE % RET_CHUNK == 0
    mod = _modulation(c, ada_w, ada_b)
    cos_t, sin_t = _rope_tables(positions)
    for layer in range(depth):
        i = layer // 2
        if layer % 2 == 0:
            w_in_p, w_uq_p, w_ukv, w_out = _pack_even_weights(ev_w_in[i], ev_w_uq[i], ev_w_ukv[i], ev_w_out[i])
            rq, rk, rv, gates, qt, kc, vt = _even_in(x, mod[layer], pre_g[layer], w_in_p, ev_q_norm_g[i], w_uq_p,
                                                     ev_kv_norm_g[i], w_ukv, cos_t, sin_t)
            ret = _retention(rq, rk, rv, ev_dec_f[i], ev_dec_b[i])
            att = _attention(qt, kc, vt)
            x = _even_out(x, ret, att, gates, w_out, post_g[layer], mod[layer])
        else:
            x = _odd_layer(x, mod[layer], pre_g[layer], post_g[layer], od_w_in[i].astype(BF16), od_b_in[i],
                           od_dw_w[i], od_dw_b[i], od_ln_g[i], od_ln_b[i], od_w_out[i].astype(BF16))
    return x
```

```python
import functools
import math

import jax
import jax.numpy as jnp
from jax import lax
from jax.experimental import pallas as pl
from jax.experimental.pallas import tpu as pltpu

F32 = jnp.float32
BF16 = jnp.bfloat16

RET_HEADS = 4
RET_QK_DIM = 64
RET_V_DIM = 128
RET_CHUNK = 128
MLA_HEADS = 4
MLA_Q_RANK = 384
MLA_KV_RANK = 256
MLA_NOPE_DIM = 128
MLA_ROPE_DIM = 64
MLA_V_DIM = 128
ROPE_DIM = 64
ROPE_BASE = 10000.0
CONV_KERNEL = 31
EPS = 1e-6

RET_WIDTH = RET_HEADS * RET_V_DIM
MLA_WIDTH = MLA_HEADS * MLA_V_DIM
RET_SCALE = RET_QK_DIM ** -0.5
LOG2E = math.log2(math.e)
ATT_SCALE_LOG2 = (MLA_NOPE_DIM + MLA_ROPE_DIM) ** -0.5 * LOG2E

LANES = 128
MXU_DIM = 256
ATT_QK_PAD = MXU_DIM
ROPE_PACK = LANES // (ROPE_DIM // 2)
CONV_HALO = 16

TOKEN_TILE = 512
EVEN_IN_TILE = 1024
WIDE_TOKEN_TILE = 1024
ATT_Q_TILE = 1024
ATT_KV_TILE = 2 * TOKEN_TILE
ATT_PV_CHUNK = MXU_DIM
ATT_VT_ROWS = MLA_V_DIM
ATT_MAX_INIT = -1e30
CONV_ROW_CHUNK = 128
ODD_COL_BLOCK = MXU_DIM
RET_CHUNKS_PER_ITER = 32

V7X_VMEM_BYTES = 64 * 1024 * 1024
VMEM_LIMIT_BYTES = V7X_VMEM_BYTES - 8 * 1024 * 1024

_C_RQ = 0
_C_RK = _C_RQ + RET_HEADS * RET_QK_DIM
_C_RV = _C_RK + RET_HEADS * RET_QK_DIM
_C_RG = _C_RV + RET_WIDTH
_C_CQ = _C_RG + RET_WIDTH
_C_CKV = _C_CQ + MLA_Q_RANK
_C_KR = _C_CKV + MLA_KV_RANK
_C_MG = _C_KR + LANES
_C_END = _C_MG + MLA_WIDTH


def _cparams(semantics):
    return pltpu.CompilerParams(dimension_semantics=semantics, vmem_limit_bytes=VMEM_LIMIT_BYTES)


def _const_spec(shape):
    nd = len(shape)
    return pl.BlockSpec(shape, lambda *_: (0,) * nd, pipeline_mode=pl.Buffered(1))


def _silu(v):
    return v * jax.nn.sigmoid(v)


def _rms(v, g):
    return v * lax.rsqrt(jnp.mean(v * v, axis=-1, keepdims=True) + EPS) * g


def _mod_kernel(c_ref, w_ref, b_ref, o_ref):
    c = c_ref[...]
    o_ref[...] = jnp.dot(_silu(c).astype(BF16), w_ref[...].astype(BF16),
                         preferred_element_type=F32) + b_ref[...]


def _modulation(c, ada_w, ada_b):
    depth, d, d3 = ada_w.shape
    b = c.shape[0]
    rows = -(-b // 8) * 8
    c_pad = jnp.pad(c, ((0, rows - b), (0, 0)))
    out = pl.pallas_call(
        _mod_kernel,
        grid=(depth, d3 // d),
        in_specs=[pl.BlockSpec((rows, d), lambda l, j: (0, 0)),
                  pl.BlockSpec((None, d, d), lambda l, j: (l, 0, j)),
                  pl.BlockSpec((None, 1, d), lambda l, j: (l, 0, j))],
        out_specs=pl.BlockSpec((None, rows, d), lambda l, j: (l, 0, j)),
        out_shape=jax.ShapeDtypeStruct((depth, rows, d3), F32),
        compiler_params=_cparams(("arbitrary", "arbitrary")),
        name="adaln_mod",
    )(c_pad, ada_w, ada_b.reshape(depth, 1, d3))
    return out[:, :b].reshape(depth, b, d3 // d, d)


def _rope_table_kernel(pos_ref, invf_ref, cos_ref, sin_ref):
    rows = pos_ref.shape[0]
    half = ROPE_DIM // 2
    ang = pos_ref[...].astype(F32) * invf_ref[...]
    lane = lax.broadcasted_iota(jnp.int32, ang.shape, 1)
    for table, dst in ((jnp.cos(ang), cos_ref), (jnp.sin(ang), sin_ref)):
        for q in range(ROPE_PACK):
            v = pltpu.roll(table, LANES - half * q, 1) if q else table
            v = jnp.where(lane < half, v, 0.0)
            v = v + pltpu.roll(v, half, 1)
            v = v + pltpu.roll(v, 2 * half, 1)
            dst[pl.ds(q, rows, stride=ROPE_PACK), :] = v


def _rope_tables(positions):
    b, s = positions.shape
    half = ROPE_DIM // 2
    inv_freq = ROPE_BASE ** (-jnp.arange(0, ROPE_DIM, 2, dtype=F32) / ROPE_DIM)
    invf = jnp.tile(inv_freq, ROPE_PACK).reshape(1, LANES)
    rows = s // ROPE_PACK
    pos = jnp.repeat(positions.reshape(b, rows, ROPE_PACK), half, axis=-1)
    tr = min(TOKEN_TILE, rows)
    return pl.pallas_call(
        _rope_table_kernel,
        grid=(b, rows // tr),
        in_specs=[pl.BlockSpec((None, tr, LANES), lambda i, t: (i, t, 0)),
                  pl.BlockSpec((1, LANES), lambda i, t: (0, 0))],
        out_specs=[pl.BlockSpec((None, ROPE_PACK * tr, LANES), lambda i, t: (i, t, 0))] * 2,
        out_shape=[jax.ShapeDtypeStruct((b, s, LANES), F32)] * 2,
        compiler_params=_cparams(("arbitrary", "arbitrary")),
        name="rope_tables",
    )(pos, invf)


def _rope_slab(v, cos, sin, first_half):
    half = ROPE_DIM // 2
    partner = jnp.where(first_half, -pltpu.roll(v, LANES - half, 1), pltpu.roll(v, half, 1))
    return v * cos + partner * sin


def _even_in_kernel(x_ref, mod_ref, pg_ref, win_ref, qg_ref, wuq_ref, kvg_ref, wukv_ref,
                    cos_ref, sin_ref,
                    rq_ref, rk_ref, rv_ref, gate_ref, qt_ref, kc_ref, vt_ref):
    x = x_ref[...]
    h = (_rms(x, pg_ref[...]) * (1.0 + mod_ref[1:2, :]) + mod_ref[0:1, :]).astype(BF16)

    def proj(lo, hi):
        return jnp.dot(h, win_ref[:, lo:hi], preferred_element_type=F32)

    cos = cos_ref[...]
    sin = sin_ref[...]
    lane = lax.broadcasted_iota(jnp.int32, cos.shape, 1)
    first_half = (lane % ROPE_DIM) < (ROPE_DIM // 2)
    rope = functools.partial(_rope_slab, cos=cos, sin=sin, first_half=first_half)

    z_lat = proj(_C_CQ, _C_MG)
    cqn = _rms(z_lat[:, :MLA_Q_RANK], qg_ref[...])
    q = jnp.dot(cqn.astype(BF16), wuq_ref[...], preferred_element_type=F32)
    kvn = _rms(z_lat[:, MLA_Q_RANK:MLA_Q_RANK + MLA_KV_RANK], kvg_ref[...])
    kv = jnp.dot(kvn.astype(BF16), wukv_ref[...], preferred_element_type=F32)
    k_rope = rope(z_lat[:, _C_KR - _C_CQ:]).astype(BF16)

    z_ret = proj(_C_RQ, _C_CQ)
    for j in range(RET_HEADS * RET_QK_DIM // LANES):
        sl = slice(j * LANES, (j + 1) * LANES)
        rq_ref[:, sl] = rope(z_ret[:, _C_RQ + j * LANES:_C_RQ + (j + 1) * LANES]).astype(BF16)
        rk_ref[:, sl] = (rope(z_ret[:, _C_RK + j * LANES:_C_RK + (j + 1) * LANES]) * RET_SCALE).astype(BF16)
    rv_ref[...] = z_ret[:, _C_RV:_C_RG].astype(BF16)
    gate_ref[:, :RET_WIDTH] = _silu(z_ret[:, _C_RG:_C_CQ]).astype(BF16)
    gate_ref[:, RET_WIDTH:] = _silu(proj(_C_MG, _C_END)).astype(BF16)

    nope_w = MLA_HEADS * MLA_NOPE_DIM
    q_rope = [rope(q[:, nope_w + j * LANES:nope_w + (j + 1) * LANES]) for j in range(2)]
    low = lane < ROPE_DIM
    for hd in range(MLA_HEADS):
        qn = q[:, hd * MLA_NOPE_DIM:(hd + 1) * MLA_NOPE_DIM] * ATT_SCALE_LOG2
        slab = q_rope[hd // 2]
        if hd % 2:
            slab = pltpu.roll(slab, ROPE_DIM, 1)
        qr = jnp.where(low, slab, 0.0) * ATT_SCALE_LOG2
        qt_ref[hd, :MLA_NOPE_DIM, :] = qn.T.astype(BF16)
        qt_ref[hd, MLA_NOPE_DIM:, :] = qr.T.astype(BF16)
        base = hd * (MLA_NOPE_DIM + MLA_V_DIM)
        kc_ref[hd, :, :MLA_NOPE_DIM] = kv[:, base:base + MLA_NOPE_DIM].astype(BF16)
        kc_ref[hd, :, MLA_NOPE_DIM:] = k_rope
        v_t = kv[:, base + MLA_NOPE_DIM:base + MLA_NOPE_DIM + MLA_V_DIM].T.astype(BF16)
        for blk in range(x.shape[0] // TOKEN_TILE):
            vt_ref[hd, blk, :MLA_V_DIM, :] = v_t[:, blk * TOKEN_TILE:(blk + 1) * TOKEN_TILE]


def _even_in(x, mod, pre_g, w_in_p, q_norm_g, w_uq_p, kv_norm_g, w_ukv, cos_t, sin_t):
    b, s, d = x.shape
    ts = math.gcd(s, EVEN_IN_TILE)
    nt = s // ts
    vt_blocks = ts // TOKEN_TILE
    hq = RET_HEADS * RET_QK_DIM
    tile3 = lambda w: pl.BlockSpec((None, ts, w), lambda i, t: (i, t, 0))
    out_shapes = [
        jax.ShapeDtypeStruct((b, s, hq), BF16),
        jax.ShapeDtypeStruct((b, s, hq), BF16),
        jax.ShapeDtypeStruct((b, s, RET_WIDTH), BF16),
        jax.ShapeDtypeStruct((b, s, RET_WIDTH + MLA_WIDTH), BF16),
        jax.ShapeDtypeStruct((b, MLA_HEADS, ATT_QK_PAD, s), BF16),
        jax.ShapeDtypeStruct((b, MLA_HEADS, s, ATT_QK_PAD), BF16),
        jax.ShapeDtypeStruct((b, MLA_HEADS, s // TOKEN_TILE, ATT_VT_ROWS, TOKEN_TILE), BF16),
    ]
    out_specs = [
        tile3(hq), tile3(hq), tile3(RET_WIDTH), tile3(RET_WIDTH + MLA_WIDTH),
        pl.BlockSpec((None, MLA_HEADS, ATT_QK_PAD, ts), lambda i, t: (i, 0, 0, t)),
        pl.BlockSpec((None, MLA_HEADS, ts, ATT_QK_PAD), lambda i, t: (i, 0, t, 0)),
        pl.BlockSpec((None, MLA_HEADS, vt_blocks, ATT_VT_ROWS, TOKEN_TILE), lambda i, t: (i, 0, t, 0, 0)),
    ]
    return pl.pallas_call(
        _even_in_kernel,
        grid=(b, nt),
        in_specs=[tile3(d),
                  pl.BlockSpec((None, 3, d), lambda i, t: (i, 0, 0)),
                  _const_spec((1, d)),
                  _const_spec(w_in_p.shape),
                  _const_spec((1, MLA_Q_RANK)),
                  _const_spec(w_uq_p.shape),
                  _const_spec((1, MLA_KV_RANK)),
                  _const_spec(w_ukv.shape),
                  tile3(LANES), tile3(LANES)],
        out_specs=out_specs,
        out_shape=out_shapes,
        compiler_params=_cparams(("arbitrary", "arbitrary")),
        name="even_in",
    )(x, mod, pre_g.reshape(1, d), w_in_p, q_norm_g.reshape(1, -1), w_uq_p,
      kv_norm_g.reshape(1, -1), w_ukv, cos_t, sin_t)


def _retention_kernel(dec_ref, q_ref, k_ref, v_ref, o_ref, sf_ref, kvb_ref, *, n_chunks, per_iter):
    c = RET_CHUNK
    dk = RET_QK_DIM
    lg = jax.nn.log_sigmoid(dec_ref[...])
    ri = lax.broadcasted_iota(jnp.int32, (c, LANES), 0)
    ci = lax.broadcasted_iota(jnp.int32, (c, LANES), 1)
    rif = ri.astype(F32)
    diff = rif - ci.astype(F32)
    row_h0 = ri < dk
    lane_h0 = ci < dk

    def dmat(lgf, lgb):
        return jnp.where(diff >= 0, jnp.exp(lgf * jnp.maximum(diff, 0.0)), jnp.exp(lgb * jnp.maximum(-diff, 0.0)))

    dec = jnp.concatenate([dmat(lg[0:1], lg[2:3]), dmat(lg[1:2], lg[3:4])], axis=0)
    qwf = jnp.concatenate([jnp.exp(lg[0:1] * (rif + 1.0)), jnp.exp(lg[1:2] * (rif + 1.0))], axis=0)
    qwb = jnp.concatenate([jnp.exp(lg[2:3] * (c - rif)), jnp.exp(lg[3:4] * (c - rif))], axis=0)
    lgf_lane = jnp.where(lane_h0, lg[0:1], lg[1:2])
    lgb_lane = jnp.where(lane_h0, lg[2:3], lg[3:4])
    kwf = jnp.exp(lgf_lane * (c - 1.0 - rif))
    kwb = jnp.exp(lgb_lane * rif)
    cdf = jnp.where(row_h0, jnp.exp(lg[0:1] * c), jnp.exp(lg[1:2] * c))
    cdb = jnp.where(row_h0, jnp.exp(lg[2:3] * c), jnp.exp(lg[3:4] * c))

    def rows_of(n):
        return pl.ds(pl.multiple_of(n * c, c), c)

    def pick_heads(m):
        return jnp.where(row_h0, m[:, :RET_V_DIM], m[:, RET_V_DIM:])

    def fwd_body(n, state_f):
        rows = rows_of(n)
        k = k_ref[rows, :].astype(F32)
        kw_t = jnp.concatenate([k * kwf, k * kwb], axis=1).T.astype(BF16)
        kv = jnp.dot(kw_t, v_ref[rows, :], preferred_element_type=F32)
        sf_ref[n] = state_f.astype(BF16)
        kvb_ref[n] = pick_heads(kv[c:, :])
        return state_f * cdf + pick_heads(kv[:c, :])

    def grouped(body):
        def group(i, state):
            for u in range(per_iter):
                state = body(i * per_iter + u, state)
            return state
        return group

    lax.fori_loop(0, n_chunks // per_iter, grouped(fwd_body), jnp.zeros((c, LANES), F32))

    def norm(o):
        mu = jnp.mean(o, axis=-1, keepdims=True)
        var = jnp.mean(jnp.square(o - mu), axis=-1, keepdims=True)
        return (o - mu) * lax.rsqrt(var + EPS)

    def bwd_body(t, state_b):
        n = n_chunks - 1 - t
        rows = rows_of(n)
        q = q_ref[rows, :].astype(F32)
        qm = jnp.concatenate([jnp.where(lane_h0, q, 0.0), jnp.where(lane_h0, 0.0, q)], axis=0)
        s = lax.dot_general(qm.astype(BF16), k_ref[rows, :], (((1,), (1,)), ((), ())),
                            preferred_element_type=F32)
        p = (s * dec).astype(BF16)
        v = v_ref[rows, :]
        inner0 = jnp.dot(p[:c], v[:, :RET_V_DIM], preferred_element_type=F32)
        inner1 = jnp.dot(p[c:], v[:, RET_V_DIM:], preferred_element_type=F32)
        lhs = jnp.concatenate([qm * qwf, qm * qwb], axis=1).astype(BF16)
        rhs = jnp.concatenate([sf_ref[n], state_b.astype(BF16)], axis=0)
        cross = jnp.dot(lhs, rhs, preferred_element_type=F32)
        o_ref[rows, :RET_V_DIM] = norm(inner0 + cross[:c]).astype(BF16)
        o_ref[rows, RET_V_DIM:] = norm(inner1 + cross[c:]).astype(BF16)
        return state_b * cdb + kvb_ref[n]

    lax.fori_loop(0, n_chunks // per_iter, grouped(bwd_body), jnp.zeros((c, LANES), F32))


def _retention(rq, rk, rv, dec_f, dec_b):
    b, s, _ = rq.shape
    n_chunks = s // RET_CHUNK
    pairs = RET_HEADS // 2
    dec = jnp.concatenate([dec_f.reshape(pairs, 2), dec_b.reshape(pairs, 2)], axis=1)
    dec = jnp.broadcast_to(dec[:, :, None], (pairs, 4, LANES)).astype(F32)
    seq = lambda w: pl.BlockSpec((None, s, w), lambda i, p: (i, 0, p))
    return pl.pallas_call(
        functools.partial(_retention_kernel, n_chunks=n_chunks, per_iter=math.gcd(n_chunks, RET_CHUNKS_PER_ITER)),
        grid=(b, pairs),
        in_specs=[pl.BlockSpec((None, 4, LANES), lambda i, p: (p, 0, 0)),
                  seq(2 * RET_QK_DIM), seq(2 * RET_QK_DIM), seq(2 * RET_V_DIM)],
        out_specs=seq(2 * RET_V_DIM),
        out_shape=jax.ShapeDtypeStruct((b, s, RET_WIDTH), BF16),
        scratch_shapes=[pltpu.VMEM((n_chunks, RET_CHUNK, LANES), BF16),
                        pltpu.VMEM((n_chunks, RET_CHUNK, LANES), F32)],
        compiler_params=_cparams(("arbitrary", "arbitrary")),
        name="retention",
    )(dec, rq, rk, rv)


def _attention_kernel(qt_ref, qt_next_ref, kc_ref, vt_ref, o_ref, s_ref, *, n_kv, tk):
    qt = qt_ref[...]
    tq = qt.shape[1]
    ck = ATT_PV_CHUNK
    vt_per_kv = tk // TOKEN_TILE
    chunks_per_vt = TOKEN_TILE // ck

    def scores(j, q):
        k = kc_ref[pl.ds(pl.multiple_of(j * tk, tk), tk), :]
        return jnp.dot(k, q, preferred_element_type=F32)

    def softmax_pv(slot, j, carry):
        m, l, acc = carry
        m_new = jnp.maximum(m, jnp.max(s_ref[slot], axis=0, keepdims=True))
        alpha = jnp.exp2(m - m_new)
        l = alpha * l
        acc = alpha * acc
        for c in range(tk // ck):
            p = jnp.exp2(s_ref[slot, c * ck:(c + 1) * ck, :] - m_new)
            l = l + jnp.sum(p, axis=0, keepdims=True)
            lo = (c % chunks_per_vt) * ck
            vt = vt_ref[j * vt_per_kv + c // chunks_per_vt, :, lo:lo + ck]
            acc = acc + jnp.dot(vt, p.astype(BF16), preferred_element_type=F32)
        return m_new, l, acc

    def pair(j, carry, j_ahead, q_ahead):
        s_ref[1] = scores(j + 1, qt)
        carry = softmax_pv(0, j, carry)
        s_ref[0] = scores(j_ahead, q_ahead)
        return softmax_pv(1, j + 1, carry)

    @pl.when(pl.program_id(2) == 0)
    def _():
        s_ref[0] = scores(0, qt)

    carry = (jnp.full((1, tq), ATT_MAX_INIT, F32), jnp.zeros((1, tq), F32), jnp.zeros((ATT_VT_ROWS, tq), F32))
    carry = lax.fori_loop(0, n_kv // 2 - 1, lambda i, c: pair(2 * i, c, 2 * i + 2, qt), carry)
    _, l, acc = pair(n_kv - 2, carry, 0, qt_next_ref[...])
    o_ref[...] = (acc / l).T.astype(BF16)


def _attention(qt, kc, vt):
    b, h, _, s = qt.shape
    n_vt = vt.shape[2]
    tq = min(ATT_Q_TILE, s)
    tk = ATT_KV_TILE if s % (2 * ATT_KV_TILE) == 0 else TOKEN_TILE
    assert s % (2 * tk) == 0 and s % tq == 0
    nq = s // tq
    return pl.pallas_call(
        functools.partial(_attention_kernel, n_kv=s // tk, tk=tk),
        grid=(b, h, nq),
        in_specs=[pl.BlockSpec((None, None, ATT_QK_PAD, tq), lambda i, j, t: (i, j, 0, t)),
                  pl.BlockSpec((None, None, ATT_QK_PAD, tq), lambda i, j, t: (i, j, 0, jnp.minimum(t + 1, nq - 1))),
                  pl.BlockSpec((None, None, s, ATT_QK_PAD), lambda i, j, t: (i, j, 0, 0)),
                  pl.BlockSpec((None, None, n_vt, ATT_VT_ROWS, TOKEN_TILE), lambda i, j, t: (i, j, 0, 0, 0))],
        out_specs=pl.BlockSpec((None, tq, MLA_V_DIM), lambda i, j, t: (i, t, j)),
        out_shape=jax.ShapeDtypeStruct((b, s, MLA_WIDTH), BF16),
        scratch_shapes=[pltpu.VMEM((2, tk, tq), F32)],
        compiler_params=_cparams(("arbitrary", "arbitrary", "arbitrary")),
        name="mla_attention",
    )(qt, qt, kc, vt)


def _even_out_kernel(x_ref, ret_ref, att_ref, gate_ref, w_ref, pg_ref, mod_ref, o_ref):
    m_ret = ret_ref[...] * gate_ref[:, :RET_WIDTH]
    m_att = att_ref[...] * gate_ref[:, RET_WIDTH:]
    y = (jnp.dot(m_ret, w_ref[:RET_WIDTH, :], preferred_element_type=F32)
         + jnp.dot(m_att, w_ref[RET_WIDTH:, :], preferred_element_type=F32))
    o_ref[...] = x_ref[...] + mod_ref[2:3, :] * _rms(y, pg_ref[...])


def _even_out(x, ret, att, gates, w_out, post_g, mod):
    b, s, d = x.shape
    ts = math.gcd(s, WIDE_TOKEN_TILE)
    tile3 = lambda w: pl.BlockSpec((None, ts, w), lambda i, t: (i, t, 0))
    return pl.pallas_call(
        _even_out_kernel,
        grid=(b, s // ts),
        in_specs=[tile3(d), tile3(RET_WIDTH), tile3(MLA_WIDTH), tile3(RET_WIDTH + MLA_WIDTH),
                  _const_spec(w_out.shape), _const_spec((1, d)),
                  pl.BlockSpec((None, 3, d), lambda i, t: (i, 0, 0))],
        out_specs=tile3(d),
        out_shape=jax.ShapeDtypeStruct((b, s, d), F32),
        compiler_params=_cparams(("arbitrary", "arbitrary")),
        name="even_out",
    )(x, ret, att, gates, w_out, post_g.reshape(1, d), mod)


def _odd_kernel(xp_ref, x_ref, xn_ref, mod_ref, pg_ref, win_ref, bin_ref, dww_ref, dwb_ref,
                lng_ref, lnb_ref, wout_ref, postg_ref, o_ref, u_ref, c_ref, *, n_tiles):
    t = pl.program_id(1)
    ts, d = x_ref.shape
    halo = CONV_HALO
    x = x_ref[...]
    xa = jnp.concatenate([xp_ref[...], x, xn_ref[...]], axis=0)
    h = (_rms(xa, pg_ref[...]) * (1.0 + mod_ref[1:2, :]) + mod_ref[0:1, :]).astype(BF16)
    row = lax.broadcasted_iota(jnp.int32, (ts + 2 * halo, 1), 0)
    inside = jnp.logical_and(jnp.logical_or(row >= halo, t > 0),
                             jnp.logical_or(row < ts + halo, t < n_tiles - 1))
    off = halo - CONV_KERNEL // 2
    gate = _silu(jnp.dot(h[halo:halo + ts], win_ref[:, 2 * d:], preferred_element_type=F32) + bin_ref[:, 2 * d:])

    for cb in range(0, d, ODD_COL_BLOCK):
        cols = slice(cb, cb + ODD_COL_BLOCK)
        gcols = slice(d + cb, d + cb + ODD_COL_BLOCK)
        za = jnp.dot(h, win_ref[:, cols], preferred_element_type=F32) + bin_ref[:, cols]
        zb = jnp.dot(h, win_ref[:, gcols], preferred_element_type=F32) + bin_ref[:, gcols]
        u = jnp.where(inside, za * jax.nn.sigmoid(zb), 0.0)
        for j in range(ODD_COL_BLOCK // LANES):
            sl = cb // LANES + j
            cs = slice(sl * LANES, (sl + 1) * LANES)
            u_ref[sl] = u[:, j * LANES:(j + 1) * LANES]
            for r0 in range(0, ts, CONV_ROW_CHUNK):
                acc = jnp.broadcast_to(dwb_ref[:, cs], (CONV_ROW_CHUNK, LANES))
                for tap in range(CONV_KERNEL):
                    acc = acc + u_ref[sl, pl.ds(r0 + off + tap, CONV_ROW_CHUNK), :] * dww_ref[tap:tap + 1, cs]
                c_ref[pl.ds(r0, CONV_ROW_CHUNK), cs] = acc

    cv = c_ref[...]
    mu = jnp.mean(cv, axis=-1, keepdims=True)
    var = jnp.mean(jnp.square(cv - mu), axis=-1, keepdims=True)
    ln = (cv - mu) * lax.rsqrt(var + EPS) * lng_ref[...] + lnb_ref[...]
    m = (_silu(ln) * gate).astype(BF16)
    y = jnp.dot(m, wout_ref[...], preferred_element_type=F32)
    o_ref[...] = x + mod_ref[2:3, :] * _rms(y, postg_ref[...])


def _odd_layer(x, mod, pre_g, post_g, w_in, b_in, dw_w, dw_b, ln_g, ln_b, w_out):
    b, s, d = x.shape
    ts = math.gcd(s, WIDE_TOKEN_TILE)
    nt = s // ts
    hb = ts // CONV_HALO
    n_hb = s // CONV_HALO
    row = lambda v: v.reshape(1, -1)
    return pl.pallas_call(
        functools.partial(_odd_kernel, n_tiles=nt),
        grid=(b, nt),
        in_specs=[pl.BlockSpec((None, CONV_HALO, d), lambda i, t: (i, jnp.maximum(t * hb - 1, 0), 0)),
                  pl.BlockSpec((None, ts, d), lambda i, t: (i, t, 0)),
                  pl.BlockSpec((None, CONV_HALO, d), lambda i, t: (i, jnp.minimum((t + 1) * hb, n_hb - 1), 0)),
                  pl.BlockSpec((None, 3, d), lambda i, t: (i, 0, 0)),
                  _const_spec((1, d)), _const_spec(w_in.shape), _const_spec((1, 3 * d)),
                  _const_spec(dw_w.shape), _const_spec((1, d)), _const_spec((1, d)), _const_spec((1, d)),
                  _const_spec(w_out.shape), _const_spec((1, d))],
        out_specs=pl.BlockSpec((None, ts, d), lambda i, t: (i, t, 0)),
        out_shape=jax.ShapeDtypeStruct((b, s, d), F32),
        scratch_shapes=[pltpu.VMEM((d // LANES, ts + 2 * CONV_HALO, LANES), F32), pltpu.VMEM((ts, d), F32)],
        compiler_params=_cparams(("arbitrary", "arbitrary")),
        name="odd_layer",
    )(x, x, x, mod, row(pre_g), w_in, row(b_in), dw_w, row(dw_b), row(ln_g), row(ln_b), w_out, row(post_g))


def _pack_even_weights(w_in, w_uq, w_ukv, w_out):
    d = w_in.shape[0]
    hq = RET_HEADS * RET_QK_DIM
    sizes = (hq, hq, RET_WIDTH, RET_WIDTH, MLA_Q_RANK, MLA_KV_RANK, MLA_ROPE_DIM, MLA_WIDTH)
    pieces, start = [], 0
    for size in sizes:
        pieces.append(w_in[:, start:start + size])
        start += size
    pieces.insert(7, jnp.zeros((d, LANES - MLA_ROPE_DIM), w_in.dtype))
    w_in_p = jnp.concatenate(pieces, axis=1).astype(BF16)
    wq = w_uq.reshape(MLA_Q_RANK, MLA_HEADS, MLA_NOPE_DIM + MLA_ROPE_DIM)
    w_uq_p = jnp.concatenate([wq[:, :, :MLA_NOPE_DIM].reshape(MLA_Q_RANK, -1),
                              wq[:, :, MLA_NOPE_DIM:].reshape(MLA_Q_RANK, -1)], axis=1).astype(BF16)
    return w_in_p, w_uq_p, w_ukv.astype(BF16), w_out.astype(BF16)


def kernel(x, c, positions, ada_w, ada_b, pre_g, post_g, ev_w_in, ev_dec_f, ev_dec_b, ev_q_norm_g, ev_w_uq, ev_kv_norm_g, ev_w_ukv, ev_w_out, od_w_in, od_b_in, od_dw_w, od_dw_b, od_ln_g, od_ln_b, od_w_out):
    depth = ada_w.shape[0]
    s = x.shape[1]
    assert s % TOKEN_TILE == 0 and TOKEN_TILE % RET_CHUNK == 0
    mod = _modulation(c, ada_w, ada_b)
    cos_t, sin_t = _rope_tables(positions)
    for layer in range(depth):
        i = layer // 2
        if layer % 2 == 0:
            w_in_p, w_uq_p, w_ukv, w_out = _pack_even_weights(ev_w_in[i], ev_w_uq[i], ev_w_ukv[i], ev_w_out[i])
            rq, rk, rv, gates, qt, kc, vt = _even_in(x, mod[layer], pre_g[layer], w_in_p, ev_q_norm_g[i], w_uq_p,
                                                     ev_kv_norm_g[i], w_ukv, cos_t, sin_t)
            ret = _retention(rq, rk, rv, ev_dec_f[i], ev_dec_b[i])
            att = _attention(qt, kc, vt)
            x = _even_out(x, ret, att, gates, w_out, post_g[layer], mod[layer])
        else:
            x = _odd_layer(x, mod[layer], pre_g[layer], post_g[layer], od_w_in[i].astype(BF16), od_b_in[i],
                           od_dw_w[i], od_dw_b[i], od_ln_g[i], od_ln_b[i], od_w_out[i].astype(BF16))
    return x
```
